```python
import math
import jax, jax.numpy as jnp
from jax import lax
import numpy as np

D_MODEL = 1024
BATCH = 8
SEQ = 2048
DEPTH = 4

N_A_LAYERS = DEPTH // 2
N_B_LAYERS = DEPTH - N_A_LAYERS
MIX_W = D_MODEL
MEM_LEN = 256
MEM_HEADS = 4
MEM_HEAD_DIM = 64
MEM_W = MEM_HEADS * MEM_HEAD_DIM
SEQ_MIX_W = MIX_W - MEM_W
HGRN_EXPAND = 128
HGRN_HEADS = SEQ_MIX_W // HGRN_EXPAND
HGRN_DK = HGRN_EXPAND
HGRN_DV = SEQ_MIX_W // HGRN_HEADS
HGRN_CHUNK = 64
SB_HEAD_DIM = 64
SB_HEADS = SEQ_MIX_W // SB_HEAD_DIM
SB_BLOCK = 128
A_IN_W = 4 * SEQ_MIX_W + MEM_W
B_IN_W = SEQ_MIX_W + MEM_W
N_GROUPS = 4
EXPERTS_PER_GROUP = 4
N_EXPERTS = N_GROUPS * EXPERTS_PER_GROUP
TOP_K_IN_GROUP = 2
D_EXPERT = 256
DEEPNORM_ALPHA = (2 * DEPTH) ** 0.25
DEEPNORM_BETA = (8 * DEPTH) ** -0.25
LN_EPS = 1e-5
RMS_EPS = 1e-6

kernel_name = "yoco_hgrn2_stickbreaking_hmoe_deepnorm"


def layer_norm(x, g, b):
    xf = x.astype(jnp.float32)
    mu = jnp.mean(xf, axis=-1, keepdims=True)
    var = jnp.mean(jnp.square(xf - mu), axis=-1, keepdims=True)
    y = (xf - mu) * lax.rsqrt(var + LN_EPS) * g.astype(jnp.float32) + b.astype(jnp.float32)
    return y.astype(x.dtype)


def post_norm(x, y, g, b):
    return layer_norm(DEEPNORM_ALPHA * x + y.astype(x.dtype), g, b)


def gla_chunkwise(q, k, v, logf):
    B, S, H, dk = q.shape
    dv = v.shape[-1]
    nc = S // HGRN_CHUNK

    def to_chunks(t):
        return t.reshape(B, nc, HGRN_CHUNK, H, t.shape[-1]).transpose(1, 0, 3, 2, 4)

    qc, kc, vc = to_chunks(q), to_chunks(k), to_chunks(v)
    G = jnp.cumsum(to_chunks(logf), axis=3)
    incl = jnp.tril(jnp.ones((HGRN_CHUNK, HGRN_CHUNK), dtype=bool))

    def step(state, inp):
        q_, k_, v_, G_ = inp
        diff = G_[:, :, :, None, :] - G_[:, :, None, :, :]
        decay = jnp.exp(jnp.where(incl[:, :, None], diff, -jnp.inf))
        scores = jnp.einsum('bhtd,bhsd,bhtsd->bhts', q_, k_, decay)
        out = (jnp.einsum('bhts,bhse->bhte', scores, v_)
               + jnp.einsum('bhtd,bhde->bhte', q_ * jnp.exp(G_), state))
        g_last = G_[:, :, -1]
        state = (jnp.exp(g_last)[..., None] * state
                 + jnp.einsum('bhsd,bhse->bhde', k_ * jnp.exp(g_last[:, :, None] - G_), v_))
        return state, out

    s0 = jnp.zeros((B, H, dk, dv), jnp.float32)
    _, out = lax.scan(step, s0, (qc, kc, vc, G))
    return out.transpose(1, 0, 3, 2, 4).reshape(B, S, H, dv)


def hgrn2_mix(q, f_raw, i_in, g, lb, gnorm):
    B, S, _ = q.shape
    qh = (jax.nn.silu(q.astype(jnp.float32)) * HGRN_DK ** -0.5).reshape(B, S, HGRN_HEADS, HGRN_DK)
    lbf = lb.astype(jnp.float32)
    logf = jnp.logaddexp(jnp.log(lbf), jnp.log1p(-lbf) + jax.nn.log_sigmoid(f_raw.astype(jnp.float32)))
    logf = logf.reshape(B, S, HGRN_HEADS, HGRN_DK)
    kh = -jnp.expm1(logf)
    vh = i_in.astype(jnp.float32).reshape(B, S, HGRN_HEADS, HGRN_DV)
    o = gla_chunkwise(qh, kh, vh, logf)
    o = o * lax.rsqrt(jnp.mean(jnp.square(o), axis=-1, keepdims=True) + RMS_EPS)
    o = o * gnorm.astype(jnp.float32).reshape(HGRN_HEADS, HGRN_DV)
    o = o.reshape(B, S, SEQ_MIX_W) * jax.nn.silu(g.astype(jnp.float32))
    return o


def stick_breaking_attention(q, k, v):
    B, S, _ = q.shape
    qh = q.astype(jnp.float32).reshape(B, S, SB_HEADS, SB_HEAD_DIM).transpose(0, 2, 1, 3)
    scale = 1.0 / math.sqrt(SB_HEAD_DIM)
    outs = []
    for blk in range(S // SB_BLOCK):
        start, end = blk * SB_BLOCK, (blk + 1) * SB_BLOCK
        qb = qh[:, :, start:end]
        kb, vb = k[:, :, :end], v[:, :, :end]
        z = jnp.einsum('bhtd,bhsd->bhts', qb, kb) * scale
        t_pos = start + jnp.arange(SB_BLOCK)
        s_pos = jnp.arange(end)
        mask = s_pos[None, :] < t_pos[:, None]
        neg_log_1m_beta = jnp.where(mask, jax.nn.softplus(z), 0.0)
        rest = lax.cumsum(neg_log_1m_beta, axis=3, reverse=True) - neg_log_1m_beta
        log_a = jax.nn.log_sigmoid(z) - rest
        a = jnp.where(mask, jnp.exp(log_a), 0.0)
        outs.append(jnp.einsum('bhts,bhsd->bhtd', a, vb))
    o = jnp.concatenate(outs, axis=2)
    return o.transpose(0, 2, 1, 3).reshape(B, S, SEQ_MIX_W)


def memory_attention(qm, mem_k, mem_v):
    B, S, _ = qm.shape
    qh = qm.astype(jnp.float32).reshape(B, S, MEM_HEADS, MEM_HEAD_DIM)
    scores = jnp.einsum('bthd,bmhd->bhtm', qh, mem_k.astype(jnp.float32)) / math.sqrt(MEM_HEAD_DIM)
    p = jax.nn.softmax(scores, axis=-1)
    o = jnp.einsum('bhtm,bmhd->bthd', p, mem_v.astype(jnp.float32))
    return o.reshape(B, S, MEM_W)


def hier_moe(x, w_group, b_group, w_router, b_router, w_gate, w_up, w_down):
    B, S, D = x.shape
    xf = x.reshape(B * S, D)
    T = xf.shape[0]
    probs_g = jax.nn.softmax((xf @ w_group + b_group).astype(jnp.float32), axis=-1)
    p_top, g_idx = lax.top_k(probs_g, 1)
    logits_e = (xf @ w_router + b_router).astype(jnp.float32).reshape(T, N_GROUPS, EXPERTS_PER_GROUP)
    in_group = jnp.take_along_axis(logits_e, g_idx[:, :, None], axis=1)[:, 0]
    probs_in = jax.nn.softmax(in_group, axis=-1)
    w_top, e_local = lax.top_k(probs_in, TOP_K_IN_GROUP)
    w_top = w_top / jnp.sum(w_top, axis=-1, keepdims=True)
    ids = g_idx * EXPERTS_PER_GROUP + e_local
    gates = jnp.einsum('tk,tke->te', p_top * w_top, jax.nn.one_hot(ids, N_EXPERTS, dtype=jnp.float32))
    h = jax.nn.silu(jnp.einsum('td,edf->tef', xf, w_gate)) * jnp.einsum('td,edf->tef', xf, w_up)
    y = jnp.einsum('tef,efd->td', h * gates[:, :, None].astype(h.dtype), w_down)
    return y.reshape(B, S, D)


def setup_inputs(seed: int = 0) -> dict:
    key = jax.random.key(seed)
    ks = iter(jax.random.split(key, 40))

    def nrm(shape, scale):
        return jax.random.normal(next(ks), shape, jnp.float32) * scale

    d_s = D_MODEL ** -0.5
    x = nrm((BATCH, SEQ, D_MODEL), 1.0)
    mem = nrm((BATCH, MEM_LEN, D_MODEL), 1.0)
    a_w_in = jnp.concatenate([
        nrm((N_A_LAYERS, D_MODEL, SEQ_MIX_W), d_s),
        nrm((N_A_LAYERS, D_MODEL, SEQ_MIX_W), d_s),
        nrm((N_A_LAYERS, D_MODEL, SEQ_MIX_W), d_s * DEEPNORM_BETA),
        nrm((N_A_LAYERS, D_MODEL, SEQ_MIX_W), d_s),
        nrm((N_A_LAYERS, D_MODEL, MEM_W), d_s),
    ], axis=-1)
    a_lower_bounds = nrm((N_A_LAYERS, SEQ_MIX_W), 0.1)
    a_gnorm = 1.0 + nrm((N_A_LAYERS, SEQ_MIX_W), 0.02)
    b_w_in = nrm((N_B_LAYERS, D_MODEL, B_IN_W), d_s)
    w_kv_shared = jnp.concatenate([nrm((D_MODEL, SEQ_MIX_W), d_s),
                                   nrm((D_MODEL, SEQ_MIX_W), d_s * DEEPNORM_BETA)], axis=-1)
    w_mem_kv = jnp.concatenate([nrm((DEPTH, D_MODEL, MEM_W), d_s),
                                nrm((DEPTH, D_MODEL, MEM_W), d_s * DEEPNORM_BETA)], axis=-1)
    w_o = nrm((DEPTH, MIX_W, D_MODEL), MIX_W ** -0.5 * DEEPNORM_BETA)
    ln_mix_g = 1.0 + nrm((DEPTH, D_MODEL), 0.02)
    ln_mix_b = nrm((DEPTH, D_MODEL), 0.02)
    ln_ffn_g = 1.0 + nrm((DEPTH, D_MODEL), 0.02)
    ln_ffn_b = nrm((DEPTH, D_MODEL), 0.02)
    w_group = nrm((DEPTH, D_MODEL, N_GROUPS), d_s)
    b_group = nrm((DEPTH, N_GROUPS), 0.01)
    w_router = nrm((DEPTH, D_MODEL, N_EXPERTS), d_s)
    b_router = nrm((DEPTH, N_EXPERTS), 0.01)
    w_gate = nrm((DEPTH, N_EXPERTS, D_MODEL, D_EXPERT), d_s)
    w_up = nrm((DEPTH, N_EXPERTS, D_MODEL, D_EXPERT), d_s)
    w_down = nrm((DEPTH, N_EXPERTS, D_EXPERT, D_MODEL), D_EXPERT ** -0.5 * DEEPNORM_BETA)
    return {"x": x, "mem": mem, "a_w_in": a_w_in, "a_lower_bounds": a_lower_bounds,
            "a_gnorm": a_gnorm, "b_w_in": b_w_in, "w_kv_shared": w_kv_shared,
            "w_mem_kv": w_mem_kv, "w_o": w_o, "ln_mix_g": ln_mix_g, "ln_mix_b": ln_mix_b,
            "ln_ffn_g": ln_ffn_g, "ln_ffn_b": ln_ffn_b, "w_group": w_group, "b_group": b_group,
            "w_router": w_router, "b_router": b_router, "w_gate": w_gate, "w_up": w_up,
            "w_down": w_down}


def reference(x, mem, a_w_in, a_lower_bounds, a_gnorm, b_w_in, w_kv_shared, w_mem_kv, w_o,
              ln_mix_g, ln_mix_b, ln_ffn_g, ln_ffn_b, w_group, b_group, w_router, b_router,
              w_gate, w_up, w_down):
    B, S, _ = x.shape
    lb_all = jnp.cumsum(jax.nn.softmax(a_lower_bounds.astype(jnp.float32), axis=0), axis=0)
    lb_all = lb_all - lb_all[0]
    shared_k = None
    shared_v = None
    for layer in range(DEPTH):
        mem_kv = mem @ w_mem_kv[layer]
        mem_k = mem_kv[..., :MEM_W].reshape(B, MEM_LEN, MEM_HEADS, MEM_HEAD_DIM)
        mem_v = mem_kv[..., MEM_W:].reshape(B, MEM_LEN, MEM_HEADS, MEM_HEAD_DIM)
        if layer < N_A_LAYERS:
            h = x @ a_w_in[layer]
            q, f_raw, i_in, g, qm = jnp.split(
                h, [SEQ_MIX_W, 2 * SEQ_MIX_W, 3 * SEQ_MIX_W, 4 * SEQ_MIX_W], axis=-1)
            seq_out = hgrn2_mix(q, f_raw, i_in, g, lb_all[layer], a_gnorm[layer])
        else:
            if layer == N_A_LAYERS:
                kv = x @ w_kv_shared
                shared_k = kv[..., :SEQ_MIX_W].astype(jnp.float32).reshape(
                    B, S, SB_HEADS, SB_HEAD_DIM).transpose(0, 2, 1, 3)
                shared_v = kv[..., SEQ_MIX_W:].astype(jnp.float32).reshape(
                    B, S, SB_HEADS, SB_HEAD_DIM).transpose(0, 2, 1, 3)
            h = x @ b_w_in[layer - N_A_LAYERS]
            q, qm = jnp.split(h, [SEQ_MIX_W], axis=-1)
            seq_out = stick_breaking_attention(q, shared_k, shared_v)
        mem_out = memory_attention(qm, mem_k, mem_v)
        mixed = jnp.concatenate([seq_out.astype(x.dtype), mem_out.astype(x.dtype)], axis=-1)
        x = post_norm(x, mixed @ w_o[layer], ln_mix_g[layer], ln_mix_b[layer])
        moe_out = hier_moe(x, w_group[layer], b_group[layer], w_router[layer], b_router[layer],
                           w_gate[layer], w_up[layer], w_down[layer])
        x = post_norm(x, moe_out, ln_ffn_g[layer], ln_ffn_b[layer])
    return x
```

```python
import functools
import math

import numpy as np
import jax
import jax.numpy as jnp
from jax import lax
from jax.experimental import pallas as pl
from jax.experimental.pallas import tpu as pltpu

F32 = jnp.float32
BF16 = jnp.bfloat16

D_MODEL = 1024
DEPTH = 4
N_A_LAYERS = DEPTH // 2
MEM_LEN = 256
MEM_HEADS = 4
HEAD_DIM = 64
MEM_W = MEM_HEADS * HEAD_DIM
SEQ_MIX_W = D_MODEL - MEM_W
HGRN_DK = 128
HGRN_HEADS = SEQ_MIX_W // HGRN_DK
SB_HEADS = SEQ_MIX_W // HEAD_DIM
N_GROUPS = 4
EXPERTS_PER_GROUP = 4
N_EXPERTS = N_GROUPS * EXPERTS_PER_GROUP
D_EXPERT = 256
DEEPNORM_ALPHA = (2 * DEPTH) ** 0.25
LN_EPS = 1e-5
RMS_EPS = 1e-6

LANES = 128
HGRN_C = 128
SB_T = 128
MIB = 1024 * 1024


def _cparams(semantics, vmem_mib):
    return pltpu.CompilerParams(dimension_semantics=semantics, vmem_limit_bytes=vmem_mib * MIB)


def _dot(a, b):
    return jnp.dot(a, b, preferred_element_type=F32)


def _dot_nt(a, b):
    return lax.dot_general(a, b, (((1,), (1,)), ((), ())), preferred_element_type=F32)


def _dot_tn(a, b):
    return lax.dot_general(a, b, (((0,), (0,)), ((), ())), preferred_element_type=F32)


def _split_bf16(x):
    hi = x.astype(BF16)
    lo = (x - hi.astype(F32)).astype(BF16)
    return hi, lo


def _layer_norm(r, g, b):
    mu = jnp.mean(r, axis=-1, keepdims=True)
    d = r - mu
    var = jnp.mean(d * d, axis=-1, keepdims=True)
    return d * lax.rsqrt(var + LN_EPS) * g + b


def _mm_kernel(x_ref, w_ref, o_ref):
    o_ref[...] = _dot(x_ref[...].astype(BF16), w_ref[...])


def _matmul(x, w, tm, tn, name):
    M, K = x.shape
    N = w.shape[1]
    assert M % tm == 0 and N % tn == 0
    return pl.pallas_call(
        _mm_kernel,
        grid=(N // tn, M // tm),
        in_specs=[pl.BlockSpec((tm, K), lambda j, i: (i, 0)),
                  pl.BlockSpec((K, tn), lambda j, i: (0, j))],
        out_specs=pl.BlockSpec((tm, tn), lambda j, i: (i, j)),
        out_shape=jax.ShapeDtypeStruct((M, N), F32),
        compiler_params=_cparams(("parallel", "parallel"), 40),
        name=name,
    )(x, w)


def _hgrn_levels():
    m = HGRN_C // 2
    out = []
    while m >= 1:
        out.append(m)
        m //= 2
    return out


def _hgrn_sum_matrix():
    C = HGRN_C
    r = np.arange(C)[:, None]
    c = np.arange(C)[None, :]
    blocks = [(c <= r), (c > r)]
    for m in _hgrn_levels():
        p = r % (2 * m)
        mid = r - p + m - 1
        upper = (p >= m) & (c > mid) & (c <= r)
        lower = (p < m) & (c > r) & (c <= mid)
        blocks.append(upper | lower)
    return np.concatenate(blocks, axis=0).astype(np.float32)


def _hgrn_kernel(layer, q_ref, f_ref, i_ref, g_ref, lbp_ref, gn_ref, sum_ref, o_ref, st_ref):
    C = HGRN_C

    @pl.when(pl.program_id(2) == 0)
    def _():
        st_ref[...] = jnp.zeros_like(st_ref)

    a = lbp_ref[...]
    ea = jnp.exp(a - jnp.max(a, axis=0, keepdims=True))
    pa = ea / jnp.sum(ea, axis=0, keepdims=True)
    lb = jnp.zeros((1, LANES), F32)
    for l in range(1, layer + 1):
        lb = lb + pa[l:l + 1]

    fr = f_ref[...]
    logsig = jnp.minimum(fr, 0.0) - jnp.log1p(jnp.exp(-jnp.abs(fr)))
    la = jnp.log(lb)
    lc = jnp.log1p(-lb) + logsig
    logf = jnp.maximum(la, lc) + jnp.log1p(jnp.exp(-jnp.abs(la - lc)))
    k = 1.0 - jnp.exp(logf)

    hi, lo = _split_bf16(logf)
    e2 = _dot(sum_ref[...], jnp.concatenate([hi, lo], axis=1))
    dec = jnp.exp(e2[:, :LANES] + e2[:, LANES:])
    dec_g = dec[0:C]
    dec_suf = dec[C:2 * C]

    qr = q_ref[...]
    q = qr * jax.nn.sigmoid(qr) * (HGRN_DK ** -0.5)
    v = i_ref[...].astype(BF16)

    t_idx = lax.broadcasted_iota(jnp.int32, (C, C), 0)
    s_idx = lax.broadcasted_iota(jnp.int32, (C, C), 1)
    split = t_idx ^ s_idx
    sc = None
    levels = _hgrn_levels()
    for li in range(len(levels) - 1, -1, -1):
        m = levels[li]
        d = dec[(2 + li) * C:(3 + li) * C]
        p = _dot_nt((q * d).astype(BF16), (k * d).astype(BF16))
        sc = p if sc is None else jnp.where(split >= m, p, sc)
    sc = jnp.where(s_idx < t_idx, sc, 0.0)
    sc = jnp.where(s_idx == t_idx, jnp.sum(q * k, axis=-1, keepdims=True), sc)

    st = st_ref[...]
    o = _dot(sc.astype(BF16), v) + _dot_nt((q * dec_g).astype(BF16), st.astype(BF16))
    st_ref[...] = st * dec_g[C - 1:C, :] + _dot_tn(v, (k * dec_suf).astype(BF16))

    o = o * lax.rsqrt(jnp.mean(o * o, axis=-1, keepdims=True) + RMS_EPS)
    gr = g_ref[...]
    o_ref[...] = o * gn_ref[...] * (gr * jax.nn.sigmoid(gr))


def _hgrn(h3, a_lower_bounds, gnorm, layer):
    B, S, _ = h3.shape
    C = HGRN_C
    H = HGRN_HEADS
    sum_mat = jnp.asarray(_hgrn_sum_matrix(), dtype=BF16)

    def col(off):
        return pl.BlockSpec((None, C, LANES), lambda b, h, c: (b, c, off + h))

    return pl.pallas_call(
        functools.partial(_hgrn_kernel, layer),
        grid=(B, H, S // C),
        in_specs=[col(0), col(H), col(2 * H), col(3 * H),
                  pl.BlockSpec((N_A_LAYERS, LANES), lambda b, h, c: (0, h)),
                  pl.BlockSpec((1, LANES), lambda b, h, c: (0, h)),
                  pl.BlockSpec(sum_mat.shape, lambda b, h, c: (0, 0))],
        out_specs=pl.BlockSpec((None, C, LANES), lambda b, h, c: (b, c, h)),
        out_shape=jax.ShapeDtypeStruct((B, S, SEQ_MIX_W), F32),
        scratch_shapes=[pltpu.VMEM((LANES, HGRN_DK), F32)],
        compiler_params=_cparams(("parallel", "parallel", "arbitrary"), 32),
        name="hgrn2",
    )(h3, h3, h3, h3, a_lower_bounds, gnorm.reshape(1, SEQ_MIX_W), sum_mat)


def _sb_kernel(q_ref, k_ref, v_ref, m_ref, o_ref):
    T = SB_T
    qi = pl.program_id(2)
    q = q_ref[...] * (1.0 / math.sqrt(HEAD_DIM))
    later = m_ref[...]
    row = lax.broadcasted_iota(jnp.int32, (T, T), 0)
    colm = lax.broadcasted_iota(jnp.int32, (T, T), 1)
    causal = colm < row
    outs = []
    for h in range(LANES // HEAD_DIM):
        lanes = slice(h * HEAD_DIM, (h + 1) * HEAD_DIM)
        qh = q[:, lanes].astype(BF16)

        def block(kb, masked, acc, carry):
            start = pl.multiple_of(kb * T, T)
            kblk = k_ref[pl.ds(start, T), lanes].astype(BF16)
            vblk = v_ref[pl.ds(start, T), lanes].astype(BF16)
            z = _dot_nt(qh, kblk)
            sp = jnp.maximum(z, 0.0) + jnp.log1p(jnp.exp(-jnp.abs(z)))
            spm = jnp.where(causal, sp, 0.0) if masked else sp
            hi, lo = _split_bf16(spm)
            rest = _dot(hi, later) + _dot(lo, later) + carry
            a = jnp.exp(z - sp - rest)
            if masked:
                a = jnp.where(causal, a, 0.0)
            acc = acc + _dot(a.astype(BF16), vblk)
            carry = carry + jnp.sum(spm, axis=1, keepdims=True)
            return acc, carry

        acc, carry = block(qi, True, jnp.zeros((T, HEAD_DIM), F32), jnp.zeros((T, 1), F32))
        acc, carry = lax.fori_loop(
            0, qi, lambda i, st: block(qi - 1 - i, False, st[0], st[1]), (acc, carry))
        outs.append(acc)
    o_ref[...] = jnp.concatenate(outs, axis=1)


def _sb_attention(h3, kv3):
    B, S, _ = h3.shape
    T = SB_T
    pairs = SEQ_MIX_W // LANES
    later = jnp.asarray(np.arange(T)[:, None] > np.arange(T)[None, :], dtype=BF16)
    return pl.pallas_call(
        _sb_kernel,
        grid=(B, pairs, S // T),
        in_specs=[pl.BlockSpec((None, T, LANES), lambda b, p, i: (b, i, p)),
                  pl.BlockSpec((None, S, LANES), lambda b, p, i: (b, 0, p)),
                  pl.BlockSpec((None, S, LANES), lambda b, p, i: (b, 0, pairs + p)),
                  pl.BlockSpec((T, T), lambda b, p, i: (0, 0))],
        out_specs=pl.BlockSpec((None, T, LANES), lambda b, p, i: (b, i, p)),
        out_shape=jax.ShapeDtypeStruct((B, S, SEQ_MIX_W), F32),
        compiler_params=_cparams(("parallel", "parallel", "arbitrary"), 32),
        name="stick_breaking",
    )(h3, kv3, kv3, later)


def _mem_kernel(q_ref, kv_ref, o_ref):
    q = q_ref[...] * (1.0 / math.sqrt(HEAD_DIM))
    outs = []
    for h in range(MEM_HEADS):
        lanes = slice(h * HEAD_DIM, (h + 1) * HEAD_DIM)
        vl = slice(MEM_W + h * HEAD_DIM, MEM_W + (h + 1) * HEAD_DIM)
        s = _dot_nt(q[:, lanes].astype(BF16), kv_ref[:, lanes].astype(BF16))
        e = jnp.exp(s - jnp.max(s, axis=-1, keepdims=True))
        den = jnp.sum(e, axis=-1, keepdims=True)
        outs.append(_dot(e.astype(BF16), kv_ref[:, vl].astype(BF16)) / den)
    o_ref[...] = jnp.concatenate(outs, axis=1)


def _mem_attention(h3, mem_kv3, q_col_block, tm):
    B, S, _ = h3.shape
    return pl.pallas_call(
        _mem_kernel,
        grid=(B, S // tm),
        in_specs=[pl.BlockSpec((None, tm, MEM_W), lambda b, i: (b, i, q_col_block)),
                  pl.BlockSpec((None, MEM_LEN, 2 * MEM_W), lambda b, i: (b, 0, 0))],
        out_specs=pl.BlockSpec((None, tm, MEM_W), lambda b, i: (b, i, 0)),
        out_shape=jax.ShapeDtypeStruct((B, S, MEM_W), F32),
        compiler_params=_cparams(("parallel", "parallel"), 32),
        name="mem_attention",
    )(h3, mem_kv3)


def _oproj_kernel(seq_ref, mem_ref, x_ref, wa_ref, wb_ref, g_ref, b_ref, o_ref):
    y = _dot(seq_ref[...].astype(BF16), wa_ref[...]) + _dot(mem_ref[...].astype(BF16), wb_ref[...])
    o_ref[...] = _layer_norm(DEEPNORM_ALPHA * x_ref[...] + y, g_ref[...], b_ref[...])


def _oproj_ln(seq, memo, x, w_o, g, b, tm):
    T = x.shape[0]
    wa = w_o[:SEQ_MIX_W]
    wb = w_o[SEQ_MIX_W:]
    return pl.pallas_call(
        _oproj_kernel,
        grid=(T // tm,),
        in_specs=[pl.BlockSpec((tm, SEQ_MIX_W), lambda i: (i, 0)),
                  pl.BlockSpec((tm, MEM_W), lambda i: (i, 0)),
                  pl.BlockSpec((tm, D_MODEL), lambda i: (i, 0)),
                  pl.BlockSpec((SEQ_MIX_W, D_MODEL), lambda i: (0, 0)),
                  pl.BlockSpec((MEM_W, D_MODEL), lambda i: (0, 0)),
                  pl.BlockSpec((1, D_MODEL), lambda i: (0, 0)),
                  pl.BlockSpec((1, D_MODEL), lambda i: (0, 0))],
        out_specs=pl.BlockSpec((tm, D_MODEL), lambda i: (i, 0)),
        out_shape=jax.ShapeDtypeStruct((T, D_MODEL), F32),
        compiler_params=_cparams(("parallel",), 40),
        name="oproj_ln",
    )(seq, memo, x, wa, wb, g.reshape(1, D_MODEL), b.reshape(1, D_MODEL))


def _router_kernel(x_ref, whi_ref, wlo_ref, bias_ref, o_ref):
    xh, xl = _split_bf16(x_ref[...])
    whi = whi_ref[...]
    lg = _dot(xh, whi) + _dot(xh, wlo_ref[...]) + _dot(xl, whi) + bias_ref[...]
    lane = lax.broadcasted_iota(jnp.int32, lg.shape, 1)
    lane_f = lane.astype(F32)
    neg = -jnp.inf
    far = float(LANES)

    lgg = jnp.where((lane >= N_EXPERTS) & (lane < N_EXPERTS + N_GROUPS), lg, neg)
    gmax = jnp.max(lgg, axis=-1, keepdims=True)
    p_top = 1.0 / jnp.sum(jnp.exp(lgg - gmax), axis=-1, keepdims=True)
    g_lane = jnp.min(jnp.where(lgg == gmax, lane_f, far), axis=-1, keepdims=True)
    first = (g_lane - float(N_EXPERTS)) * float(EXPERTS_PER_GROUP)

    le = jnp.where((lane_f >= first) & (lane_f < first + float(EXPERTS_PER_GROUP)), lg, neg)
    m1 = jnp.max(le, axis=-1, keepdims=True)
    i1 = jnp.min(jnp.where(le == m1, lane_f, far), axis=-1, keepdims=True)
    le2 = jnp.where(lane_f == i1, neg, le)
    m2 = jnp.max(le2, axis=-1, keepdims=True)
    i2 = jnp.min(jnp.where(le2 == m2, lane_f, far), axis=-1, keepdims=True)
    e2 = jnp.exp(m2 - m1)
    den = 1.0 + e2
    o_ref[...] = p_top * (jnp.where(lane_f == i1, 1.0 / den, 0.0) + jnp.where(lane_f == i2, e2 / den, 0.0))


def _router(x, w_group, b_group, w_router, b_router, tm):
    T = x.shape[0]
    pad = LANES - N_EXPERTS - N_GROUPS
    w = jnp.concatenate([w_router, w_group, jnp.zeros((D_MODEL, pad), F32)], axis=1)
    bias = jnp.concatenate([b_router, b_group, jnp.zeros((pad,), F32)]).reshape(1, LANES)
    whi, wlo = _split_bf16(w)
    return pl.pallas_call(
        _router_kernel,
        grid=(T // tm,),
        in_specs=[pl.BlockSpec((tm, D_MODEL), lambda i: (i, 0)),
                  pl.BlockSpec((D_MODEL, LANES), lambda i: (0, 0)),
                  pl.BlockSpec((D_MODEL, LANES), lambda i: (0, 0)),
                  pl.BlockSpec((1, LANES), lambda i: (0, 0))],
        out_specs=pl.BlockSpec((tm, LANES), lambda i: (i, 0)),
        out_shape=jax.ShapeDtypeStruct((T, LANES), F32),
        compiler_params=_cparams(("parallel",), 32),
        name="router",
    )(x, whi, wlo, bias)


def _moe_kernel(x_ref, gates_ref, wgu_ref, wd_ref, g_ref, b_ref, o_ref, xb_ref, acc_ref):
    e = pl.program_id(1)

    @pl.when(e == 0)
    def _():
        xb_ref[...] = x_ref[...].astype(BF16)
        acc_ref[...] = jnp.zeros_like(acc_ref)

    hgu = _dot(xb_ref[...], wgu_ref[...])
    hg = hgu[:, :D_EXPERT]
    hu = hgu[:, D_EXPERT:]
    gates = gates_ref[...]
    lane = lax.broadcasted_iota(jnp.int32, gates.shape, 1)
    ge = jnp.sum(jnp.where(lane == e, gates, 0.0), axis=-1, keepdims=True)
    hh = hg * jax.nn.sigmoid(hg) * hu * ge
    acc_ref[...] += _dot(hh.astype(BF16), wd_ref[...])

    @pl.when(e == N_EXPERTS - 1)
    def _():
        o_ref[...] = _layer_norm(DEEPNORM_ALPHA * x_ref[...] + acc_ref[...], g_ref[...], b_ref[...])


def _moe_ln(x, gates, wgu, wd, g, b, tm):
    T = x.shape[0]
    return pl.pallas_call(
        _moe_kernel,
        grid=(T // tm, N_EXPERTS),
        in_specs=[pl.BlockSpec((tm, D_MODEL), lambda i, e: (i, 0)),
                  pl.BlockSpec((tm, LANES), lambda i, e: (i, 0)),
                  pl.BlockSpec((None, D_MODEL, 2 * D_EXPERT), lambda i, e: (e, 0, 0)),
                  pl.BlockSpec((None, D_EXPERT, D_MODEL), lambda i, e: (e, 0, 0)),
                  pl.BlockSpec((1, D_MODEL), lambda i, e: (0, 0)),
                  pl.BlockSpec((1, D_MODEL), lambda i, e: (0, 0))],
        out_specs=pl.BlockSpec((tm, D_MODEL), lambda i, e: (i, 0)),
        out_shape=jax.ShapeDtypeStruct((T, D_MODEL), F32),
        scratch_shapes=[pltpu.VMEM((tm, D_MODEL), BF16), pltpu.VMEM((tm, D_MODEL), F32)],
        compiler_params=_cparams(("parallel", "arbitrary"), 48),
        name="moe_ln",
    )(x, gates, wgu, wd, g.reshape(1, D_MODEL), b.reshape(1, D_MODEL))


def kernel(x, mem, a_w_in, a_lower_bounds, a_gnorm, b_w_in, w_kv_shared, w_mem_kv, w_o,
           ln_mix_g, ln_mix_b, ln_ffn_g, ln_ffn_b, w_group, b_group, w_router, b_router,
           w_gate, w_up, w_down):
    B, S, D = x.shape
    T = B * S
    xf = x.reshape(T, D)
    memf = mem.reshape(B * MEM_LEN, D)
    a_cols = a_w_in.shape[-1]
    b_cols = b_w_in.shape[-1]
    kv3 = None
    for layer in range(DEPTH):
        mem_kv = _matmul(memf, w_mem_kv[layer].astype(BF16), 512, 2 * MEM_W, "mem_kv")
        mem_kv3 = mem_kv.reshape(B, MEM_LEN, 2 * MEM_W)
        if layer < N_A_LAYERS:
            h = _matmul(xf, a_w_in[layer].astype(BF16), 512, a_cols // 2, "in_proj_a")
            h3 = h.reshape(B, S, a_cols)
            seq = _hgrn(h3, a_lower_bounds, a_gnorm[layer], layer)
            q_col_block = 4 * SEQ_MIX_W // MEM_W
        else:
            if layer == N_A_LAYERS:
                kv = _matmul(xf, w_kv_shared.astype(BF16), 512, SEQ_MIX_W, "kv_shared")
                kv3 = kv.reshape(B, S, 2 * SEQ_MIX_W)
            h = _matmul(xf, b_w_in[layer - N_A_LAYERS].astype(BF16), 512, b_cols, "in_proj_b")
            h3 = h.reshape(B, S, b_cols)
            seq = _sb_attention(h3, kv3)
            q_col_block = SEQ_MIX_W // MEM_W
        memo = _mem_attention(h3, mem_kv3, q_col_block, 512)
        x1 = _oproj_ln(seq.reshape(T, SEQ_MIX_W), memo.reshape(T, MEM_W), xf,
                       w_o[layer].astype(BF16), ln_mix_g[layer], ln_mix_b[layer], 512)
        gates = _router(x1, w_group[layer], b_group[layer], w_router[layer], b_router[layer], 1024)
        wgu = jnp.concatenate([w_gate[layer], w_up[layer]], axis=-1).astype(BF16)
        xf = _moe_ln(x1, gates, wgu, w_down[layer].astype(BF16),
                     ln_ffn_g[layer], ln_ffn_b[layer], 1024)
    return xf.reshape(B, S, D)
```

```python
import functools
import math

import numpy as np
import jax
import jax.numpy as jnp
from jax import lax
from jax.experimental import pallas as pl
from jax.experimental.pallas import tpu as pltpu

F32 = jnp.float32
BF16 = jnp.bfloat16

D_MODEL = 1024
DEPTH = 4
N_A_LAYERS = DEPTH // 2
MEM_LEN = 256
MEM_HEADS = 4
HEAD_DIM = 64
MEM_W = MEM_HEADS * HEAD_DIM
SEQ_MIX_W = D_MODEL - MEM_W
HGRN_DK = 128
HGRN_HEADS = SEQ_MIX_W // HGRN_DK
SB_HEADS = SEQ_MIX_W // HEAD_DIM
N_GROUPS = 4
EXPERTS_PER_GROUP = 4
N_EXPERTS = N_GROUPS * EXPERTS_PER_GROUP
D_EXPERT = 256
DEEPNORM_ALPHA = (2 * DEPTH) ** 0.25
LN_EPS = 1e-5
RMS_EPS = 1e-6

LANES = 128
HGRN_C = 128
SB_TK = 256
MIB = 1024 * 1024


def _cparams(semantics, vmem_mib):
    return pltpu.CompilerParams(dimension_semantics=semantics, vmem_limit_bytes=vmem_mib * MIB)


def _dot(a, b):
    return jnp.dot(a, b, preferred_element_type=F32)


def _dot_nt(a, b):
    return lax.dot_general(a, b, (((1,), (1,)), ((), ())), preferred_element_type=F32)


def _dot_tn(a, b):
    return lax.dot_general(a, b, (((0,), (0,)), ((), ())), preferred_element_type=F32)


def _split_bf16(x):
    hi = x.astype(BF16)
    lo = (x - hi.astype(F32)).astype(BF16)
    return hi, lo


def _layer_norm(r, g, b):
    mu = jnp.mean(r, axis=-1, keepdims=True)
    d = r - mu
    var = jnp.mean(d * d, axis=-1, keepdims=True)
    return d * lax.rsqrt(var + LN_EPS) * g + b


def _mm_kernel(x_ref, w_ref, o_ref):
    o_ref[...] = _dot(x_ref[...].astype(BF16), w_ref[...]).astype(o_ref.dtype)


def _matmul(x, w, tm, tn, name, out_dtype=F32):
    M, K = x.shape
    N = w.shape[1]
    assert M % tm == 0 and N % tn == 0
    return pl.pallas_call(
        _mm_kernel,
        grid=(N // tn, M // tm),
        in_specs=[pl.BlockSpec((tm, K), lambda j, i: (i, 0)),
                  pl.BlockSpec((K, tn), lambda j, i: (0, j))],
        out_specs=pl.BlockSpec((tm, tn), lambda j, i: (i, j)),
        out_shape=jax.ShapeDtypeStruct((M, N), out_dtype),
        compiler_params=_cparams(("parallel", "parallel"), 40),
        name=name,
    )(x, w)


def _hgrn_levels():
    m = HGRN_C // 2
    out = []
    while m >= 1:
        out.append(m)
        m //= 2
    return out


def _hgrn_sum_matrix():
    C = HGRN_C
    r = np.arange(C)[:, None]
    c = np.arange(C)[None, :]
    blocks = [(c <= r), (c > r)]
    for m in _hgrn_levels():
        p = r % (2 * m)
        mid = r - p + m - 1
        upper = (p >= m) & (c > mid) & (c <= r)
        lower = (p < m) & (c > r) & (c <= mid)
        blocks.append(upper | lower)
    return np.concatenate(blocks, axis=0).astype(np.float32)


def _hgrn_kernel(layer, q_ref, f_ref, i_ref, g_ref, lbp_ref, gn_ref, sum_ref, o_ref, st_ref):
    C = HGRN_C

    @pl.when(pl.program_id(1) == 0)
    def _():
        st_ref[...] = jnp.zeros_like(st_ref)

    a = lbp_ref[...]
    ea = jnp.exp(a - jnp.max(a, axis=0, keepdims=True))
    pa = ea / jnp.sum(ea, axis=0, keepdims=True)
    lb_all = jnp.zeros((1, SEQ_MIX_W), F32)
    for l in range(1, layer + 1):
        lb_all = lb_all + pa[l:l + 1]

    t_idx = lax.broadcasted_iota(jnp.int32, (C, C), 0)
    s_idx = lax.broadcasted_iota(jnp.int32, (C, C), 1)
    split = t_idx ^ s_idx
    levels = _hgrn_levels()
    sum_mat = sum_ref[...]

    for h in range(HGRN_HEADS):
        lanes = slice(h * LANES, (h + 1) * LANES)
        lb = lb_all[:, lanes]
        fr = f_ref[:, lanes]
        logsig = jnp.minimum(fr, 0.0) - jnp.log1p(jnp.exp(-jnp.abs(fr)))
        la = jnp.log(lb)
        lc = jnp.log1p(-lb) + logsig
        logf = jnp.maximum(la, lc) + jnp.log1p(jnp.exp(-jnp.abs(la - lc)))
        k = 1.0 - jnp.exp(logf)

        hi, lo = _split_bf16(logf)
        e2 = _dot(sum_mat, jnp.concatenate([hi, lo], axis=1))
        dec = jnp.exp(e2[:, :LANES] + e2[:, LANES:])
        dec_g = dec[0:C]
        dec_suf = dec[C:2 * C]

        qr = q_ref[:, lanes]
        q = qr * jax.nn.sigmoid(qr) * (HGRN_DK ** -0.5)
        v = i_ref[:, lanes].astype(BF16)

        sc = None
        for li in range(len(levels) - 1, -1, -1):
            d = dec[(2 + li) * C:(3 + li) * C]
            p = _dot_nt((q * d).astype(BF16), (k * d).astype(BF16))
            sc = p if sc is None else jnp.where(split >= levels[li], p, sc)
        sc = jnp.where(s_idx < t_idx, sc, 0.0)
        sc = jnp.where(s_idx == t_idx, jnp.sum(q * k, axis=-1, keepdims=True), sc)

        st = st_ref[h]
        o = _dot(sc.astype(BF16), v) + _dot_nt((q * dec_g).astype(BF16), st.astype(BF16))
        st_ref[h] = st * dec_g[C - 1:C, :] + _dot_tn(v, (k * dec_suf).astype(BF16))

        o = o * lax.rsqrt(jnp.mean(o * o, axis=-1, keepdims=True) + RMS_EPS)
        gr = g_ref[:, lanes]
        o_ref[:, lanes] = (o * gn_ref[:, lanes] * (gr * jax.nn.sigmoid(gr))).astype(o_ref.dtype)


def _hgrn(h3, a_lower_bounds, gnorm, layer):
    B, S, _ = h3.shape
    C = HGRN_C
    W = SEQ_MIX_W
    sum_mat = jnp.asarray(_hgrn_sum_matrix(), dtype=BF16)

    def col(j):
        return pl.BlockSpec((None, C, W), lambda b, c: (b, c, j))

    return pl.pallas_call(
        functools.partial(_hgrn_kernel, layer),
        grid=(B, S // C),
        in_specs=[col(0), col(1), col(2), col(3),
                  pl.BlockSpec((N_A_LAYERS, W), lambda b, c: (0, 0)),
                  pl.BlockSpec((1, W), lambda b, c: (0, 0)),
                  pl.BlockSpec(sum_mat.shape, lambda b, c: (0, 0))],
        out_specs=pl.BlockSpec((None, C, W), lambda b, c: (b, c, 0)),
        out_shape=jax.ShapeDtypeStruct((B, S, W), BF16),
        scratch_shapes=[pltpu.VMEM((HGRN_HEADS, LANES, HGRN_DK), F32)],
        compiler_params=_cparams(("parallel", "arbitrary"), 32),
        name="hgrn2",
    )(h3, h3, h3, h3, a_lower_bounds, gnorm.reshape(1, W), sum_mat)


def _sb_kernel(q_ref, k_ref, v_ref, later_ref, o_ref, acc_ref, carry_ref):
    TK = SB_TK
    TQ = 2 * TK
    qi = pl.program_id(2)
    lane = lax.broadcasted_iota(jnp.int32, (TQ, LANES), 1)
    q = q_ref[...] * (1.0 / math.sqrt(HEAD_DIM))
    zero = jnp.zeros_like(q)
    qh = [jnp.where(lane < HEAD_DIM, q, zero), jnp.where(lane >= HEAD_DIM, q, zero)]
    later = later_ref[...]
    row = lax.broadcasted_iota(jnp.int32, (TK, TK), 0)
    col = lax.broadcasted_iota(jnp.int32, (TK, TK), 1)
    causal = col < row

    def sweep(h, rows, kb, masked, carry):
        start = pl.multiple_of(kb * TK, TK)
        z = _dot_nt(qh[h][rows], k_ref[pl.ds(start, TK), :])
        sp = jnp.maximum(z, 0.0) + jnp.log(1.0 + jnp.exp(-jnp.abs(z)))
        spm = jnp.where(causal, sp, 0.0) if masked else sp
        rest = _dot(spm.astype(BF16), later) + carry
        a = jnp.exp(z - sp - rest)
        if masked:
            a = jnp.where(causal, a, 0.0)
        contrib = _dot(a.astype(BF16), v_ref[pl.ds(start, TK), :])
        return contrib, carry + jnp.sum(spm, axis=1, keepdims=True)

    top = slice(0, TK)
    bot = slice(TK, TQ)
    no_mass = jnp.zeros((TK, 1), F32)
    for h in range(2):
        acc_t, c_t = sweep(h, top, 2 * qi, True, no_mass)
        acc_b, c_b = sweep(h, bot, 2 * qi + 1, True, no_mass)
        acc_b2, c_b = sweep(h, bot, 2 * qi, False, c_b)
        acc_ref[h, top, :] = acc_t
        acc_ref[h, bot, :] = acc_b + acc_b2
        carry_ref[h, top, :] = c_t
        carry_ref[h, bot, :] = c_b

    def body(i, _):
        kb = 2 * qi - 1 - i
        for h in range(2):
            contrib, c = sweep(h, slice(0, TQ), kb, False, carry_ref[h])
            acc_ref[h] += contrib
            carry_ref[h] = c
        return 0

    lax.fori_loop(0, 2 * qi, body, 0)
    o_ref[...] = jnp.where(lane < HEAD_DIM, acc_ref[0], acc_ref[1]).astype(o_ref.dtype)


def _sb_attention(h3, kv3):
    B, S, _ = h3.shape
    TK = SB_TK
    TQ = 2 * TK
    pairs = SEQ_MIX_W // LANES
    later = jnp.asarray(np.arange(TK)[:, None] > np.arange(TK)[None, :], dtype=BF16)
    return pl.pallas_call(
        _sb_kernel,
        grid=(B, pairs, S // TQ),
        in_specs=[pl.BlockSpec((None, TQ, LANES), lambda b, p, i: (b, i, p)),
                  pl.BlockSpec((None, S, LANES), lambda b, p, i: (b, 0, p)),
                  pl.BlockSpec((None, S, LANES), lambda b, p, i: (b, 0, pairs + p)),
                  pl.BlockSpec((TK, TK), lambda b, p, i: (0, 0))],
        out_specs=pl.BlockSpec((None, TQ, LANES), lambda b, p, i: (b, i, p)),
        out_shape=jax.ShapeDtypeStruct((B, S, SEQ_MIX_W), BF16),
        scratch_shapes=[pltpu.VMEM((2, TQ, LANES), F32), pltpu.VMEM((2, TQ, 1), F32)],
        compiler_params=_cparams(("parallel", "parallel", "arbitrary"), 32),
        name="stick_breaking",
    )(h3, kv3, kv3, later)


def _mem_kernel(q_ref, kv_ref, o_ref):
    q = q_ref[...] * (1.0 / math.sqrt(HEAD_DIM))
    outs = []
    for h in range(MEM_HEADS):
        lanes = slice(h * HEAD_DIM, (h + 1) * HEAD_DIM)
        vl = slice(MEM_W + h * HEAD_DIM, MEM_W + (h + 1) * HEAD_DIM)
        s = _dot_nt(q[:, lanes].astype(BF16), kv_ref[:, lanes].astype(BF16))
        e = jnp.exp(s - jnp.max(s, axis=-1, keepdims=True))
        den = jnp.sum(e, axis=-1, keepdims=True)
        outs.append(_dot(e.astype(BF16), kv_ref[:, vl].astype(BF16)) / den)
    o_ref[...] = jnp.concatenate(outs, axis=1).astype(o_ref.dtype)


def _mem_attention(h3, mem_kv3, q_col_block, tm):
    B, S, _ = h3.shape
    return pl.pallas_call(
        _mem_kernel,
        grid=(B, S // tm),
        in_specs=[pl.BlockSpec((None, tm, MEM_W), lambda b, i: (b, i, q_col_block)),
                  pl.BlockSpec((None, MEM_LEN, 2 * MEM_W), lambda b, i: (b, 0, 0))],
        out_specs=pl.BlockSpec((None, tm, MEM_W), lambda b, i: (b, i, 0)),
        out_shape=jax.ShapeDtypeStruct((B, S, MEM_W), BF16),
        compiler_params=_cparams(("parallel", "parallel"), 32),
        name="mem_attention",
    )(h3, mem_kv3)


def _oproj_kernel(seq_ref, mem_ref, x_ref, wa_ref, wb_ref, g_ref, b_ref, o_ref):
    y = _dot(seq_ref[...].astype(BF16), wa_ref[...]) + _dot(mem_ref[...].astype(BF16), wb_ref[...])
    o_ref[...] = _layer_norm(DEEPNORM_ALPHA * x_ref[...] + y, g_ref[...], b_ref[...])


def _oproj_ln(seq, memo, x, w_o, g, b, tm):
    T = x.shape[0]
    wa = w_o[:SEQ_MIX_W]
    wb = w_o[SEQ_MIX_W:]
    return pl.pallas_call(
        _oproj_kernel,
        grid=(T // tm,),
        in_specs=[pl.BlockSpec((tm, SEQ_MIX_W), lambda i: (i, 0)),
                  pl.BlockSpec((tm, MEM_W), lambda i: (i, 0)),
                  pl.BlockSpec((tm, D_MODEL), lambda i: (i, 0)),
                  pl.BlockSpec((SEQ_MIX_W, D_MODEL), lambda i: (0, 0)),
                  pl.BlockSpec((MEM_W, D_MODEL), lambda i: (0, 0)),
                  pl.BlockSpec((1, D_MODEL), lambda i: (0, 0)),
                  pl.BlockSpec((1, D_MODEL), lambda i: (0, 0))],
        out_specs=pl.BlockSpec((tm, D_MODEL), lambda i: (i, 0)),
        out_shape=jax.ShapeDtypeStruct((T, D_MODEL), F32),
        compiler_params=_cparams(("parallel",), 40),
        name="oproj_ln",
    )(seq, memo, x, wa, wb, g.reshape(1, D_MODEL), b.reshape(1, D_MODEL))


def _router_kernel(x_ref, whi_ref, wlo_ref, bias_ref, o_ref):
    xh, xl = _split_bf16(x_ref[...])
    whi = whi_ref[...]
    lg = _dot(xh, whi) + _dot(xh, wlo_ref[...]) + _dot(xl, whi) + bias_ref[...]
    lane = lax.broadcasted_iota(jnp.int32, lg.shape, 1)
    lane_f = lane.astype(F32)
    neg = -jnp.inf
    far = float(LANES)

    lgg = jnp.where((lane >= N_EXPERTS) & (lane < N_EXPERTS + N_GROUPS), lg, neg)
    gmax = jnp.max(lgg, axis=-1, keepdims=True)
    p_top = 1.0 / jnp.sum(jnp.exp(lgg - gmax), axis=-1, keepdims=True)
    g_lane = jnp.min(jnp.where(lgg == gmax, lane_f, far), axis=-1, keepdims=True)
    first = (g_lane - float(N_EXPERTS)) * float(EXPERTS_PER_GROUP)

    le = jnp.where((lane_f >= first) & (lane_f < first + float(EXPERTS_PER_GROUP)), lg, neg)
    m1 = jnp.max(le, axis=-1, keepdims=True)
    i1 = jnp.min(jnp.where(le == m1, lane_f, far), axis=-1, keepdims=True)
    le2 = jnp.where(lane_f == i1, neg, le)
    m2 = jnp.max(le2, axis=-1, keepdims=True)
    i2 = jnp.min(jnp.where(le2 == m2, lane_f, far), axis=-1, keepdims=True)
    e2 = jnp.exp(m2 - m1)
    den = 1.0 + e2
    o_ref[...] = p_top * (jnp.where(lane_f == i1, 1.0 / den, 0.0) + jnp.where(lane_f == i2, e2 / den, 0.0))


def _router(x, w_group, b_group, w_router, b_router, tm):
    T = x.shape[0]
    pad = LANES - N_EXPERTS - N_GROUPS
    w = jnp.concatenate([w_router, w_group, jnp.zeros((D_MODEL, pad), F32)], axis=1)
    bias = jnp.concatenate([b_router, b_group, jnp.zeros((pad,), F32)]).reshape(1, LANES)
    whi, wlo = _split_bf16(w)
    return pl.pallas_call(
        _router_kernel,
        grid=(T // tm,),
        in_specs=[pl.BlockSpec((tm, D_MODEL), lambda i: (i, 0)),
                  pl.BlockSpec((D_MODEL, LANES), lambda i: (0, 0)),
                  pl.BlockSpec((D_MODEL, LANES), lambda i: (0, 0)),
                  pl.BlockSpec((1, LANES), lambda i: (0, 0))],
        out_specs=pl.BlockSpec((tm, LANES), lambda i: (i, 0)),
        out_shape=jax.ShapeDtypeStruct((T, LANES), F32),
        compiler_params=_cparams(("parallel",), 32),
        name="router",
    )(x, whi, wlo, bias)


def _moe_kernel(x_ref, gates_ref, wgu_ref, wd_ref, g_ref, b_ref, o_ref, xb_ref, acc_ref):
    e = pl.program_id(1)

    @pl.when(e == 0)
    def _():
        xb_ref[...] = x_ref[...].astype(BF16)
        acc_ref[...] = jnp.zeros_like(acc_ref)

    hgu = _dot(xb_ref[...], wgu_ref[...])
    hg = hgu[:, :D_EXPERT]
    hu = hgu[:, D_EXPERT:]
    gates = gates_ref[...]
    lane = lax.broadcasted_iota(jnp.int32, gates.shape, 1)
    ge = jnp.sum(jnp.where(lane == e, gates, 0.0), axis=-1, keepdims=True)
    hh = hg * jax.nn.sigmoid(hg) * hu * ge
    acc_ref[...] += _dot(hh.astype(BF16), wd_ref[...])

    @pl.when(e == N_EXPERTS - 1)
    def _():
        o_ref[...] = _layer_norm(DEEPNORM_ALPHA * x_ref[...] + acc_ref[...], g_ref[...], b_ref[...])


def _moe_ln(x, gates, wgu, wd, g, b, tm):
    T = x.shape[0]
    return pl.pallas_call(
        _moe_kernel,
        grid=(T // tm, N_EXPERTS),
        in_specs=[pl.BlockSpec((tm, D_MODEL), lambda i, e: (i, 0)),
                  pl.BlockSpec((tm, LANES), lambda i, e: (i, 0)),
                  pl.BlockSpec((None, D_MODEL, 2 * D_EXPERT), lambda i, e: (e, 0, 0)),
                  pl.BlockSpec((None, D_EXPERT, D_MODEL), lambda i, e: (e, 0, 0)),
                  pl.BlockSpec((1, D_MODEL), lambda i, e: (0, 0)),
                  pl.BlockSpec((1, D_MODEL), lambda i, e: (0, 0))],
        out_specs=pl.BlockSpec((tm, D_MODEL), lambda i, e: (i, 0)),
        out_shape=jax.ShapeDtypeStruct((T, D_MODEL), F32),
        scratch_shapes=[pltpu.VMEM((tm, D_MODEL), BF16), pltpu.VMEM((tm, D_MODEL), F32)],
        compiler_params=_cparams(("parallel", "arbitrary"), 48),
        name="moe_ln",
    )(x, gates, wgu, wd, g.reshape(1, D_MODEL), b.reshape(1, D_MODEL))


def kernel(x, mem, a_w_in, a_lower_bounds, a_gnorm, b_w_in, w_kv_shared, w_mem_kv, w_o,
           ln_mix_g, ln_mix_b, ln_ffn_g, ln_ffn_b, w_group, b_group, w_router, b_router,
           w_gate, w_up, w_down):
    B, S, D = x.shape
    T = B * S
    xf = x.reshape(T, D)
    memf = mem.reshape(B * MEM_LEN, D)
    a_cols = a_w_in.shape[-1]
    b_cols = b_w_in.shape[-1]
    kv3 = None
    for layer in range(DEPTH):
        mem_kv = _matmul(memf, w_mem_kv[layer].astype(BF16), 512, 2 * MEM_W, "mem_kv")
        mem_kv3 = mem_kv.reshape(B, MEM_LEN, 2 * MEM_W)
        if layer < N_A_LAYERS:
            h = _matmul(xf, a_w_in[layer].astype(BF16), 512, a_cols // 2, "in_proj_a")
            h3 = h.reshape(B, S, a_cols)
            seq = _hgrn(h3, a_lower_bounds, a_gnorm[layer], layer)
            q_col_block = 4 * SEQ_MIX_W // MEM_W
        else:
            if layer == N_A_LAYERS:
                kv = _matmul(xf, w_kv_shared.astype(BF16), 512, SEQ_MIX_W, "kv_shared", BF16)
                kv3 = kv.reshape(B, S, 2 * SEQ_MIX_W)
            h = _matmul(xf, b_w_in[layer - N_A_LAYERS].astype(BF16), 512, b_cols, "in_proj_b", BF16)
            h3 = h.reshape(B, S, b_cols)
            seq = _sb_attention(h3, kv3)
            q_col_block = SEQ_MIX_W // MEM_W
        memo = _mem_attention(h3, mem_kv3, q_col_block, 512)
        x1 = _oproj_ln(seq.reshape(T, SEQ_MIX_W), memo.reshape(T, MEM_W), xf,
                       w_o[layer].astype(BF16), ln_mix_g[layer], ln_mix_b[layer], 512)
        gates = _router(x1, w_group[layer], b_group[layer], w_router[layer], b_router[layer], 1024)
        wgu = jnp.concatenate([w_gate[layer], w_up[layer]], axis=-1).astype(BF16)
        xf = _moe_ln(x1, gates, wgu, w_down[layer].astype(BF16),
                     ln_ffn_g[layer], ln_ffn_b[layer], 1024)
    return xf.reshape(B, S, D)
```

```python
import functools
import math

import numpy as np
import jax
import jax.numpy as jnp
from jax import lax
from jax.experimental import pallas as pl
from jax.experimental.pallas import tpu as pltpu

F32 = jnp.float32
BF16 = jnp.bfloat16

D_MODEL = 1024
DEPTH = 4
N_A_LAYERS = DEPTH // 2
MEM_LEN = 256
MEM_HEADS = 4
HEAD_DIM = 64
MEM_W = MEM_HEADS * HEAD_DIM
SEQ_MIX_W = D_MODEL - MEM_W
HGRN_DK = 128
HGRN_HEADS = SEQ_MIX_W // HGRN_DK
SB_HEADS = SEQ_MIX_W // HEAD_DIM
N_GROUPS = 4
EXPERTS_PER_GROUP = 4
N_EXPERTS = N_GROUPS * EXPERTS_PER_GROUP
D_EXPERT = 256
DEEPNORM_ALPHA = (2 * DEPTH) ** 0.25
LN_EPS = 1e-5
RMS_EPS = 1e-6

LANES = 128
HGRN_C = 128
SB_TK = 256
MOE_EXPERTS_PER_STEP = 2
MIB = 1024 * 1024


def _cparams(semantics, vmem_mib):
    return pltpu.CompilerParams(dimension_semantics=semantics, vmem_limit_bytes=vmem_mib * MIB)


def _dot(a, b):
    return jnp.dot(a, b, preferred_element_type=F32)


def _dot_nt(a, b):
    return lax.dot_general(a, b, (((1,), (1,)), ((), ())), preferred_element_type=F32)


def _dot_tn(a, b):
    return lax.dot_general(a, b, (((0,), (0,)), ((), ())), preferred_element_type=F32)


def _split_bf16(x):
    hi = x.astype(BF16)
    lo = (x - hi.astype(F32)).astype(BF16)
    return hi, lo


def _layer_norm(r, g, b):
    mu = jnp.mean(r, axis=-1, keepdims=True)
    d = r - mu
    var = jnp.mean(d * d, axis=-1, keepdims=True)
    return d * lax.rsqrt(var + LN_EPS) * g + b


def _mm_kernel(x_ref, w_ref, o_ref, wb_ref):
    @pl.when(pl.program_id(1) == 0)
    def _():
        wb_ref[...] = w_ref[...].astype(BF16)

    o_ref[...] = _dot(x_ref[...].astype(BF16), wb_ref[...]).astype(o_ref.dtype)


def _matmul(x, w_stack, layer, tm, tn, name, out_dtype=F32):
    M, K = x.shape
    N = w_stack.shape[2]
    assert M % tm == 0 and N % tn == 0
    return pl.pallas_call(
        _mm_kernel,
        grid=(N // tn, M // tm),
        in_specs=[pl.BlockSpec((tm, K), lambda j, i: (i, 0)),
                  pl.BlockSpec((None, K, tn), lambda j, i: (layer, 0, j))],
        out_specs=pl.BlockSpec((tm, tn), lambda j, i: (i, j)),
        out_shape=jax.ShapeDtypeStruct((M, N), out_dtype),
        scratch_shapes=[pltpu.VMEM((K, tn), BF16)],
        compiler_params=_cparams(("parallel", "arbitrary"), 48),
        name=name,
    )(x, w_stack)


def _hgrn_levels():
    m = HGRN_C // 2
    out = []
    while m >= 1:
        out.append(m)
        m //= 2
    return out


def _hgrn_sum_matrix():
    C = HGRN_C
    r = np.arange(C)[:, None]
    c = np.arange(C)[None, :]
    blocks = [(c <= r), (c > r)]
    for m in _hgrn_levels():
        p = r % (2 * m)
        mid = r - p + m - 1
        upper = (p >= m) & (c > mid) & (c <= r)
        lower = (p < m) & (c > r) & (c <= mid)
        blocks.append(upper | lower)
    return np.concatenate(blocks, axis=0).astype(np.float32)


def _hgrn_kernel(layer, q_ref, f_ref, i_ref, g_ref, lbp_ref, gn_ref, sum_ref, o_ref, st_ref):
    C = HGRN_C

    @pl.when(pl.program_id(1) == 0)
    def _():
        st_ref[...] = jnp.zeros_like(st_ref)

    a = lbp_ref[...]
    ea = jnp.exp(a - jnp.max(a, axis=0, keepdims=True))
    pa = ea / jnp.sum(ea, axis=0, keepdims=True)
    lb_all = jnp.zeros((1, SEQ_MIX_W), F32)
    for l in range(1, layer + 1):
        lb_all = lb_all + pa[l:l + 1]

    t_idx = lax.broadcasted_iota(jnp.int32, (C, C), 0)
    s_idx = lax.broadcasted_iota(jnp.int32, (C, C), 1)
    split = t_idx ^ s_idx
    levels = _hgrn_levels()
    sum_mat = sum_ref[...]

    for h in range(HGRN_HEADS):
        lanes = slice(h * LANES, (h + 1) * LANES)
        lb = lb_all[:, lanes]
        fr = f_ref[:, lanes]
        logsig = jnp.minimum(fr, 0.0) - jnp.log1p(jnp.exp(-jnp.abs(fr)))
        la = jnp.log(lb)
        lc = jnp.log1p(-lb) + logsig
        logf = jnp.maximum(la, lc) + jnp.log1p(jnp.exp(-jnp.abs(la - lc)))
        k = 1.0 - jnp.exp(logf)

        hi, lo = _split_bf16(logf)
        e2 = _dot(sum_mat, jnp.concatenate([hi, lo], axis=1))
        dec = jnp.exp(e2[:, :LANES] + e2[:, LANES:])
        dec_g = dec[0:C]
        dec_suf = dec[C:2 * C]

        qr = q_ref[:, lanes]
        q = qr * jax.nn.sigmoid(qr) * (HGRN_DK ** -0.5)
        v = i_ref[:, lanes].astype(BF16)

        sc = None
        for li in range(len(levels) - 1, -1, -1):
            d = dec[(2 + li) * C:(3 + li) * C]
            p = _dot_nt((q * d).astype(BF16), (k * d).astype(BF16))
            sc = p if sc is None else jnp.where(split >= levels[li], p, sc)
        sc = jnp.where(s_idx < t_idx, sc, 0.0)
        sc = jnp.where(s_idx == t_idx, jnp.sum(q * k, axis=-1, keepdims=True), sc)

        st = st_ref[h]
        o = _dot(sc.astype(BF16), v) + _dot_nt((q * dec_g).astype(BF16), st.astype(BF16))
        st_ref[h] = st * dec_g[C - 1:C, :] + _dot_tn(v, (k * dec_suf).astype(BF16))

        o = o * lax.rsqrt(jnp.mean(o * o, axis=-1, keepdims=True) + RMS_EPS)
        gr = g_ref[:, lanes]
        o_ref[:, lanes] = (o * gn_ref[:, lanes] * (gr * jax.nn.sigmoid(gr))).astype(o_ref.dtype)


def _hgrn(h3, a_lower_bounds, gnorm, layer):
    B, S, _ = h3.shape
    C = HGRN_C
    W = SEQ_MIX_W
    sum_mat = jnp.asarray(_hgrn_sum_matrix(), dtype=BF16)

    def col(j):
        return pl.BlockSpec((None, C, W), lambda b, c: (b, c, j))

    return pl.pallas_call(
        functools.partial(_hgrn_kernel, layer),
        grid=(B, S // C),
        in_specs=[col(0), col(1), col(2), col(3),
                  pl.BlockSpec((N_A_LAYERS, W), lambda b, c: (0, 0)),
                  pl.BlockSpec((1, W), lambda b, c: (0, 0)),
                  pl.BlockSpec(sum_mat.shape, lambda b, c: (0, 0))],
        out_specs=pl.BlockSpec((None, C, W), lambda b, c: (b, c, 0)),
        out_shape=jax.ShapeDtypeStruct((B, S, W), BF16),
        scratch_shapes=[pltpu.VMEM((HGRN_HEADS, LANES, HGRN_DK), F32)],
        compiler_params=_cparams(("parallel", "arbitrary"), 32),
        name="hgrn2",
    )(h3, h3, h3, h3, a_lower_bounds, gnorm.reshape(1, W), sum_mat)


def _sb_kernel(q_ref, k_ref, v_ref, later_ref, o_ref, acc_ref, carry_ref):
    TK = SB_TK
    TQ = 2 * TK
    qi = pl.program_id(2)
    lane = lax.broadcasted_iota(jnp.int32, (TQ, LANES), 1)
    q = q_ref[...] * (1.0 / math.sqrt(HEAD_DIM))
    zero = jnp.zeros_like(q)
    qh = [jnp.where(lane < HEAD_DIM, q, zero), jnp.where(lane >= HEAD_DIM, q, zero)]
    later = later_ref[...]
    row = lax.broadcasted_iota(jnp.int32, (TK, TK), 0)
    col = lax.broadcasted_iota(jnp.int32, (TK, TK), 1)
    causal = col < row

    def sweep(h, rows, kb, masked, carry):
        start = pl.multiple_of(kb * TK, TK)
        z = _dot_nt(qh[h][rows], k_ref[pl.ds(start, TK), :])
        sp = jnp.maximum(z, 0.0) + jnp.log(1.0 + jnp.exp(-jnp.abs(z)))
        spm = jnp.where(causal, sp, 0.0) if masked else sp
        rest = _dot(spm.astype(BF16), later) + carry
        a = jnp.exp(z - sp - rest)
        if masked:
            a = jnp.where(causal, a, 0.0)
        contrib = _dot(a.astype(BF16), v_ref[pl.ds(start, TK), :])
        return contrib, carry + jnp.sum(spm, axis=1, keepdims=True)

    top = slice(0, TK)
    bot = slice(TK, TQ)
    no_mass = jnp.zeros((TK, 1), F32)
    for h in range(2):
        acc_t, c_t = sweep(h, top, 2 * qi, True, no_mass)
        acc_b, c_b = sweep(h, bot, 2 * qi + 1, True, no_mass)
        acc_b2, c_b = sweep(h, bot, 2 * qi, False, c_b)
        acc_ref[h, top, :] = acc_t
        acc_ref[h, bot, :] = acc_b + acc_b2
        carry_ref[h, top, :] = c_t
        carry_ref[h, bot, :] = c_b

    def body(i, _):
        for h in range(2):
            c0 = carry_ref[h]
            contrib1, c1 = sweep(h, slice(0, TQ), 2 * qi - 1 - 2 * i, False, c0)
            contrib2, c2 = sweep(h, slice(0, TQ), 2 * qi - 2 - 2 * i, False, c1)
            acc_ref[h] += contrib1 + contrib2
            carry_ref[h] = c2
        return 0

    lax.fori_loop(0, qi, body, 0)
    o_ref[...] = jnp.where(lane < HEAD_DIM, acc_ref[0], acc_ref[1]).astype(o_ref.dtype)


def _sb_attention(h3, kv3):
    B, S, _ = h3.shape
    TK = SB_TK
    TQ = 2 * TK
    pairs = SEQ_MIX_W // LANES
    later = jnp.asarray(np.arange(TK)[:, None] > np.arange(TK)[None, :], dtype=BF16)
    return pl.pallas_call(
        _sb_kernel,
        grid=(B, pairs, S // TQ),
        in_specs=[pl.BlockSpec((None, TQ, LANES), lambda b, p, i: (b, i, p)),
                  pl.BlockSpec((None, S, LANES), lambda b, p, i: (b, 0, p)),
                  pl.BlockSpec((None, S, LANES), lambda b, p, i: (b, 0, pairs + p)),
                  pl.BlockSpec((TK, TK), lambda b, p, i: (0, 0))],
        out_specs=pl.BlockSpec((None, TQ, LANES), lambda b, p, i: (b, i, p)),
        out_shape=jax.ShapeDtypeStruct((B, S, SEQ_MIX_W), BF16),
        scratch_shapes=[pltpu.VMEM((2, TQ, LANES), F32), pltpu.VMEM((2, TQ, 1), F32)],
        compiler_params=_cparams(("parallel", "parallel", "arbitrary"), 32),
        name="stick_breaking",
    )(h3, kv3, kv3, later)


def _mem_kernel(q_ref, kv_ref, o_ref):
    q = q_ref[...] * (1.0 / math.sqrt(HEAD_DIM))
    outs = []
    for h in range(MEM_HEADS):
        lanes = slice(h * HEAD_DIM, (h + 1) * HEAD_DIM)
        vl = slice(MEM_W + h * HEAD_DIM, MEM_W + (h + 1) * HEAD_DIM)
        s = _dot_nt(q[:, lanes].astype(BF16), kv_ref[:, lanes].astype(BF16))
        e = jnp.exp(s - jnp.max(s, axis=-1, keepdims=True))
        den = jnp.sum(e, axis=-1, keepdims=True)
        outs.append(_dot(e.astype(BF16), kv_ref[:, vl].astype(BF16)) / den)
    o_ref[...] = jnp.concatenate(outs, axis=1).astype(o_ref.dtype)


def _mem_attention(h3, mem_kv3, q_col_block, tm):
    B, S, _ = h3.shape
    return pl.pallas_call(
        _mem_kernel,
        grid=(B, S // tm),
        in_specs=[pl.BlockSpec((None, tm, MEM_W), lambda b, i: (b, i, q_col_block)),
                  pl.BlockSpec((None, MEM_LEN, 2 * MEM_W), lambda b, i: (b, 0, 0))],
        out_specs=pl.BlockSpec((None, tm, MEM_W), lambda b, i: (b, i, 0)),
        out_shape=jax.ShapeDtypeStruct((B, S, MEM_W), BF16),
        compiler_params=_cparams(("parallel", "parallel"), 32),
        name="mem_attention",
    )(h3, mem_kv3)


def _route(lg):
    lane = lax.broadcasted_iota(jnp.int32, lg.shape, 1)
    lane_f = lane.astype(F32)
    neg = -jnp.inf
    far = float(LANES)

    lgg = jnp.where((lane >= N_EXPERTS) & (lane < N_EXPERTS + N_GROUPS), lg, neg)
    gmax = jnp.max(lgg, axis=-1, keepdims=True)
    p_top = 1.0 / jnp.sum(jnp.exp(lgg - gmax), axis=-1, keepdims=True)
    g_lane = jnp.min(jnp.where(lgg == gmax, lane_f, far), axis=-1, keepdims=True)
    first = (g_lane - float(N_EXPERTS)) * float(EXPERTS_PER_GROUP)

    le = jnp.where((lane_f >= first) & (lane_f < first + float(EXPERTS_PER_GROUP)), lg, neg)
    m1 = jnp.max(le, axis=-1, keepdims=True)
    i1 = jnp.min(jnp.where(le == m1, lane_f, far), axis=-1, keepdims=True)
    le2 = jnp.where(lane_f == i1, neg, le)
    m2 = jnp.max(le2, axis=-1, keepdims=True)
    i2 = jnp.min(jnp.where(le2 == m2, lane_f, far), axis=-1, keepdims=True)
    e2 = jnp.exp(m2 - m1)
    den = 1.0 + e2
    return p_top * (jnp.where(lane_f == i1, 1.0 / den, 0.0) + jnp.where(lane_f == i2, e2 / den, 0.0))


def _oproj_kernel(seq_ref, mem_ref, x_ref, wo_ref, g_ref, b_ref, wr_ref, br_ref,
                  o_ref, gates_ref, wob_ref, wrh_ref, wrl_ref):
    @pl.when(pl.program_id(0) == 0)
    def _():
        wob_ref[...] = wo_ref[...].astype(BF16)
        hi, lo = _split_bf16(wr_ref[...])
        wrh_ref[...] = hi
        wrl_ref[...] = lo

    y = (_dot(seq_ref[...], wob_ref[0:SEQ_MIX_W, :]) + _dot(mem_ref[...], wob_ref[SEQ_MIX_W:, :]))
    x1 = _layer_norm(DEEPNORM_ALPHA * x_ref[...] + y, g_ref[...], b_ref[...])
    o_ref[...] = x1
    xh, xl = _split_bf16(x1)
    wrh = wrh_ref[...]
    gates_ref[...] = _route(_dot(xh, wrh) + _dot(xh, wrl_ref[...]) + _dot(xl, wrh) + br_ref[...])


def _oproj_ln_route(seq, memo, x, w_o, layer, g, b, w_group, b_group, w_router, b_router, tm):
    T = x.shape[0]
    pad = LANES - N_EXPERTS - N_GROUPS
    wr = jnp.concatenate([w_router, w_group, jnp.zeros((D_MODEL, pad), F32)], axis=1)
    br = jnp.concatenate([b_router, b_group, jnp.zeros((pad,), F32)]).reshape(1, LANES)
    const = lambda i: (0, 0)
    return pl.pallas_call(
        _oproj_kernel,
        grid=(T // tm,),
        in_specs=[pl.BlockSpec((tm, SEQ_MIX_W), lambda i: (i, 0)),
                  pl.BlockSpec((tm, MEM_W), lambda i: (i, 0)),
                  pl.BlockSpec((tm, D_MODEL), lambda i: (i, 0)),
                  pl.BlockSpec((None, D_MODEL, D_MODEL), lambda i: (layer, 0, 0)),
                  pl.BlockSpec((1, D_MODEL), const),
                  pl.BlockSpec((1, D_MODEL), const),
                  pl.BlockSpec((D_MODEL, LANES), const),
                  pl.BlockSpec((1, LANES), const)],
        out_specs=[pl.BlockSpec((tm, D_MODEL), lambda i: (i, 0)),
                   pl.BlockSpec((tm, LANES), lambda i: (i, 0))],
        out_shape=[jax.ShapeDtypeStruct((T, D_MODEL), F32),
                   jax.ShapeDtypeStruct((T, LANES), F32)],
        scratch_shapes=[pltpu.VMEM((D_MODEL, D_MODEL), BF16),
                        pltpu.VMEM((D_MODEL, LANES), BF16),
                        pltpu.VMEM((D_MODEL, LANES), BF16)],
        compiler_params=_cparams(("arbitrary",), 40),
        name="oproj_ln_route",
    )(seq, memo, x, w_o, g.reshape(1, D_MODEL), b.reshape(1, D_MODEL), wr, br)


def _moe_kernel(x_ref, gates_ref, wg_ref, wu_ref, wd_ref, g_ref, b_ref, o_ref, xb_ref, acc_ref):
    step = pl.program_id(1)

    @pl.when(step == 0)
    def _():
        xb_ref[...] = x_ref[...].astype(BF16)
        acc_ref[...] = jnp.zeros_like(acc_ref)

    xb = xb_ref[...]
    gates = gates_ref[...]
    lane = lax.broadcasted_iota(jnp.int32, gates.shape, 1)
    hs = []
    for j in range(MOE_EXPERTS_PER_STEP):
        e = step * MOE_EXPERTS_PER_STEP + j
        hg = _dot(xb, wg_ref[j].astype(BF16))
        hu = _dot(xb, wu_ref[j].astype(BF16))
        ge = jnp.sum(jnp.where(lane == e, gates, 0.0), axis=-1, keepdims=True)
        hs.append((hg * jax.nn.sigmoid(hg) * hu * ge).astype(BF16))
    wd = wd_ref[...].astype(BF16).reshape(MOE_EXPERTS_PER_STEP * D_EXPERT, D_MODEL)
    acc_ref[...] += _dot(jnp.concatenate(hs, axis=1), wd)

    @pl.when(step == pl.num_programs(1) - 1)
    def _():
        o_ref[...] = _layer_norm(DEEPNORM_ALPHA * x_ref[...] + acc_ref[...], g_ref[...], b_ref[...])


def _moe_ln(x, gates, w_gate, w_up, w_down, layer, g, b, tm):
    T = x.shape[0]
    eps = MOE_EXPERTS_PER_STEP
    const = lambda i, s: (0, 0)
    return pl.pallas_call(
        _moe_kernel,
        grid=(T // tm, N_EXPERTS // eps),
        in_specs=[pl.BlockSpec((tm, D_MODEL), lambda i, s: (i, 0)),
                  pl.BlockSpec((tm, LANES), lambda i, s: (i, 0)),
                  pl.BlockSpec((None, eps, D_MODEL, D_EXPERT), lambda i, s: (layer, s, 0, 0)),
                  pl.BlockSpec((None, eps, D_MODEL, D_EXPERT), lambda i, s: (layer, s, 0, 0)),
                  pl.BlockSpec((None, eps, D_EXPERT, D_MODEL), lambda i, s: (layer, s, 0, 0)),
                  pl.BlockSpec((1, D_MODEL), const),
                  pl.BlockSpec((1, D_MODEL), const)],
        out_specs=pl.BlockSpec((tm, D_MODEL), lambda i, s: (i, 0)),
        out_shape=jax.ShapeDtypeStruct((T, D_MODEL), F32),
        scratch_shapes=[pltpu.VMEM((tm, D_MODEL), BF16), pltpu.VMEM((tm, D_MODEL), F32)],
        compiler_params=_cparams(("parallel", "arbitrary"), 52),
        name="moe_ln",
    )(x, gates, w_gate, w_up, w_down, g.reshape(1, D_MODEL), b.reshape(1, D_MODEL))


def kernel(x, mem, a_w_in, a_lower_bounds, a_gnorm, b_w_in, w_kv_shared, w_mem_kv, w_o,
           ln_mix_g, ln_mix_b, ln_ffn_g, ln_ffn_b, w_group, b_group, w_router, b_router,
           w_gate, w_up, w_down):
    B, S, D = x.shape
    T = B * S
    xf = x.reshape(T, D)
    memf = mem.reshape(B * MEM_LEN, D)
    a_cols = a_w_in.shape[-1]
    b_cols = b_w_in.shape[-1]
    kv3 = None
    for layer in range(DEPTH):
        mem_kv = _matmul(memf, w_mem_kv, layer, 512, 2 * MEM_W, "mem_kv")
        mem_kv3 = mem_kv.reshape(B, MEM_LEN, 2 * MEM_W)
        if layer < N_A_LAYERS:
            h = _matmul(xf, a_w_in, layer, 512, a_cols // 2, "in_proj_a")
            h3 = h.reshape(B, S, a_cols)
            seq = _hgrn(h3, a_lower_bounds, a_gnorm[layer], layer)
            q_col_block = 4 * SEQ_MIX_W // MEM_W
        else:
            if layer == N_A_LAYERS:
                kv = _matmul(xf, w_kv_shared[None], 0, 512, SEQ_MIX_W, "kv_shared", BF16)
                kv3 = kv.reshape(B, S, 2 * SEQ_MIX_W)
            h = _matmul(xf, b_w_in, layer - N_A_LAYERS, 512, b_cols, "in_proj_b", BF16)
            h3 = h.reshape(B, S, b_cols)
            seq = _sb_attention(h3, kv3)
            q_col_block = SEQ_MIX_W // MEM_W
        memo = _mem_attention(h3, mem_kv3, q_col_block, 512)
        x1, gates = _oproj_ln_route(
            seq.reshape(T, SEQ_MIX_W), memo.reshape(T, MEM_W), xf, w_o, layer,
            ln_mix_g[layer], ln_mix_b[layer],
            w_group[layer], b_group[layer], w_router[layer], b_router[layer], 512)
        xf = _moe_ln(x1, gates, w_gate, w_up, w_down, layer,
                     ln_ffn_g[layer], ln_ffn_b[layer], 1024)
    return xf.reshape(B, S, D)
```

```python
import functools
import math

import numpy as np
import jax
import jax.numpy as jnp
from jax import lax
from jax.experimental import pallas as pl
from jax.experimental.pallas import tpu as pltpu

F32 = jnp.float32
BF16 = jnp.bfloat16

D_MODEL = 1024
DEPTH = 4
N_A_LAYERS = DEPTH // 2
MEM_LEN = 256
MEM_HEADS = 4
HEAD_DIM = 64
MEM_W = MEM_HEADS * HEAD_DIM
SEQ_MIX_W = D_MODEL - MEM_W
HGRN_DK = 128
HGRN_HEADS = SEQ_MIX_W // HGRN_DK
SB_HEADS = SEQ_MIX_W // HEAD_DIM
N_GROUPS = 4
EXPERTS_PER_GROUP = 4
N_EXPERTS = N_GROUPS * EXPERTS_PER_GROUP
D_EXPERT = 256
DEEPNORM_ALPHA = (2 * DEPTH) ** 0.25
LN_EPS = 1e-5
RMS_EPS = 1e-6

LANES = 128
HGRN_C = 128
SB_TK = 256
MOE_TM = 512
MOE_R = 128
MOE_MAX_PAIRS = MOE_TM // MOE_R + N_GROUPS - 1
MOE_DEST_LANE = N_EXPERTS
MIB = 1024 * 1024


def _cparams(semantics, vmem_mib):
    return pltpu.CompilerParams(dimension_semantics=semantics, vmem_limit_bytes=vmem_mib * MIB)


def _dot(a, b):
    return jnp.dot(a, b, preferred_element_type=F32)


def _dot_nt(a, b):
    return lax.dot_general(a, b, (((1,), (1,)), ((), ())), preferred_element_type=F32)


def _dot_tn(a, b):
    return lax.dot_general(a, b, (((0,), (0,)), ((), ())), preferred_element_type=F32)


def _split_bf16(x):
    hi = x.astype(BF16)
    lo = (x - hi.astype(F32)).astype(BF16)
    return hi, lo


def _layer_norm(r, g, b):
    mu = jnp.mean(r, axis=-1, keepdims=True)
    d = r - mu
    var = jnp.mean(d * d, axis=-1, keepdims=True)
    return d * lax.rsqrt(var + LN_EPS) * g + b


def _mm_kernel(x_ref, w_ref, o_ref, wb_ref):
    @pl.when(pl.program_id(1) == 0)
    def _():
        wb_ref[...] = w_ref[...].astype(BF16)

    o_ref[...] = _dot(x_ref[...].astype(BF16), wb_ref[...]).astype(o_ref.dtype)


def _matmul(x, w_stack, layer, tm, tn, name, out_dtype=F32):
    M, K = x.shape
    N = w_stack.shape[2]
    assert M % tm == 0 and N % tn == 0
    return pl.pallas_call(
        _mm_kernel,
        grid=(N // tn, M // tm),
        in_specs=[pl.BlockSpec((tm, K), lambda j, i: (i, 0)),
                  pl.BlockSpec((None, K, tn), lambda j, i: (layer, 0, j))],
        out_specs=pl.BlockSpec((tm, tn), lambda j, i: (i, j)),
        out_shape=jax.ShapeDtypeStruct((M, N), out_dtype),
        scratch_shapes=[pltpu.VMEM((K, tn), BF16)],
        compiler_params=_cparams(("parallel", "arbitrary"), 48),
        name=name,
    )(x, w_stack)


def _hgrn_levels():
    m = HGRN_C // 2
    out = []
    while m >= 1:
        out.append(m)
        m //= 2
    return out


def _hgrn_sum_matrix():
    C = HGRN_C
    r = np.arange(C)[:, None]
    c = np.arange(C)[None, :]
    blocks = [(c <= r), (c > r)]
    for m in _hgrn_levels():
        p = r % (2 * m)
        mid = r - p + m - 1
        upper = (p >= m) & (c > mid) & (c <= r)
        lower = (p < m) & (c > r) & (c <= mid)
        blocks.append(upper | lower)
    return np.concatenate(blocks, axis=0).astype(np.float32)


def _hgrn_kernel(layer, q_ref, f_ref, i_ref, g_ref, lbp_ref, gn_ref, sum_ref, o_ref, st_ref):
    C = HGRN_C

    @pl.when(pl.program_id(1) == 0)
    def _():
        st_ref[...] = jnp.zeros_like(st_ref)

    a = lbp_ref[...]
    ea = jnp.exp(a - jnp.max(a, axis=0, keepdims=True))
    pa = ea / jnp.sum(ea, axis=0, keepdims=True)
    lb_all = jnp.zeros((1, SEQ_MIX_W), F32)
    for l in range(1, layer + 1):
        lb_all = lb_all + pa[l:l + 1]

    t_idx = lax.broadcasted_iota(jnp.int32, (C, C), 0)
    s_idx = lax.broadcasted_iota(jnp.int32, (C, C), 1)
    split = t_idx ^ s_idx
    levels = _hgrn_levels()
    sum_mat = sum_ref[...]

    for h in range(HGRN_HEADS):
        lanes = slice(h * LANES, (h + 1) * LANES)
        lb = lb_all[:, lanes]
        fr = f_ref[:, lanes]
        logsig = jnp.minimum(fr, 0.0) - jnp.log1p(jnp.exp(-jnp.abs(fr)))
        la = jnp.log(lb)
        lc = jnp.log1p(-lb) + logsig
        logf = jnp.maximum(la, lc) + jnp.log1p(jnp.exp(-jnp.abs(la - lc)))
        k = 1.0 - jnp.exp(logf)

        hi, lo = _split_bf16(logf)
        e2 = _dot(sum_mat, jnp.concatenate([hi, lo], axis=1))
        dec = jnp.exp(e2[:, :LANES] + e2[:, LANES:])
        dec_g = dec[0:C]
        dec_suf = dec[C:2 * C]

        qr = q_ref[:, lanes]
        q = qr * jax.nn.sigmoid(qr) * (HGRN_DK ** -0.5)
        v = i_ref[:, lanes].astype(BF16)

        sc = None
        for li in range(len(levels) - 1, -1, -1):
            d = dec[(2 + li) * C:(3 + li) * C]
            p = _dot_nt((q * d).astype(BF16), (k * d).astype(BF16))
            sc = p if sc is None else jnp.where(split >= levels[li], p, sc)
        sc = jnp.where(s_idx < t_idx, sc, 0.0)
        sc = jnp.where(s_idx == t_idx, jnp.sum(q * k, axis=-1, keepdims=True), sc)

        st = st_ref[h]
        o = _dot(sc.astype(BF16), v) + _dot_nt((q * dec_g).astype(BF16), st.astype(BF16))
        st_ref[h] = st * dec_g[C - 1:C, :] + _dot_tn(v, (k * dec_suf).astype(BF16))

        o = o * lax.rsqrt(jnp.mean(o * o, axis=-1, keepdims=True) + RMS_EPS)
        gr = g_ref[:, lanes]
        o_ref[:, lanes] = (o * gn_ref[:, lanes] * (gr * jax.nn.sigmoid(gr))).astype(o_ref.dtype)


def _hgrn(h3, a_lower_bounds, gnorm, layer):
    B, S, _ = h3.shape
    C = HGRN_C
    W = SEQ_MIX_W
    sum_mat = jnp.asarray(_hgrn_sum_matrix(), dtype=BF16)

    def col(j):
        return pl.BlockSpec((None, C, W), lambda b, c: (b, c, j))

    return pl.pallas_call(
        functools.partial(_hgrn_kernel, layer),
        grid=(B, S // C),
        in_specs=[col(0), col(1), col(2), col(3),
                  pl.BlockSpec((N_A_LAYERS, W), lambda b, c: (0, 0)),
                  pl.BlockSpec((1, W), lambda b, c: (0, 0)),
                  pl.BlockSpec(sum_mat.shape, lambda b, c: (0, 0))],
        out_specs=pl.BlockSpec((None, C, W), lambda b, c: (b, c, 0)),
        out_shape=jax.ShapeDtypeStruct((B, S, W), BF16),
        scratch_shapes=[pltpu.VMEM((HGRN_HEADS, LANES, HGRN_DK), F32)],
        compiler_params=_cparams(("parallel", "arbitrary"), 32),
        name="hgrn2",
    )(h3, h3, h3, h3, a_lower_bounds, gnorm.reshape(1, W), sum_mat)


def _sb_kernel(q_ref, k_ref, v_ref, later_ref, o_ref, acc_ref, carry_ref):
    TK = SB_TK
    TQ = 2 * TK
    qi = pl.program_id(2)
    lane = lax.broadcasted_iota(jnp.int32, (TQ, LANES), 1)
    q = q_ref[...] * (1.0 / math.sqrt(HEAD_DIM))
    zero = jnp.zeros_like(q)
    qh = [jnp.where(lane < HEAD_DIM, q, zero), jnp.where(lane >= HEAD_DIM, q, zero)]
    later = later_ref[...]
    row = lax.broadcasted_iota(jnp.int32, (TK, TK), 0)
    col = lax.broadcasted_iota(jnp.int32, (TK, TK), 1)
    causal = col < row

    def sweep(h, rows, kb, masked, carry):
        start = pl.multiple_of(kb * TK, TK)
        z = _dot_nt(qh[h][rows], k_ref[pl.ds(start, TK), :])
        sp = jnp.maximum(z, 0.0) + jnp.log(1.0 + jnp.exp(-jnp.abs(z)))
        spm = jnp.where(causal, sp, 0.0) if masked else sp
        rest = _dot(spm.astype(BF16), later) + carry
        a = jnp.exp(z - sp - rest)
        if masked:
            a = jnp.where(causal, a, 0.0)
        contrib = _dot(a.astype(BF16), v_ref[pl.ds(start, TK), :])
        return contrib, carry + jnp.sum(spm, axis=1, keepdims=True)

    top = slice(0, TK)
    bot = slice(TK, TQ)
    no_mass = jnp.zeros((TK, 1), F32)
    for h in range(2):
        acc_t, c_t = sweep(h, top, 2 * qi, True, no_mass)
        acc_b, c_b = sweep(h, bot, 2 * qi + 1, True, no_mass)
        acc_b2, c_b = sweep(h, bot, 2 * qi, False, c_b)
        acc_ref[h, top, :] = acc_t
        acc_ref[h, bot, :] = acc_b + acc_b2
        carry_ref[h, top, :] = c_t
        carry_ref[h, bot, :] = c_b

    def body(i, _):
        for h in range(2):
            c0 = carry_ref[h]
            contrib1, c1 = sweep(h, slice(0, TQ), 2 * qi - 1 - 2 * i, False, c0)
            contrib2, c2 = sweep(h, slice(0, TQ), 2 * qi - 2 - 2 * i, False, c1)
            acc_ref[h] += contrib1 + contrib2
            carry_ref[h] = c2
        return 0

    lax.fori_loop(0, qi, body, 0)
    o_ref[...] = jnp.where(lane < HEAD_DIM, acc_ref[0], acc_ref[1]).astype(o_ref.dtype)


def _sb_attention(h3, kv3):
    B, S, _ = h3.shape
    TK = SB_TK
    TQ = 2 * TK
    pairs = SEQ_MIX_W // LANES
    later = jnp.asarray(np.arange(TK)[:, None] > np.arange(TK)[None, :], dtype=BF16)
    return pl.pallas_call(
        _sb_kernel,
        grid=(B, pairs, S // TQ),
        in_specs=[pl.BlockSpec((None, TQ, LANES), lambda b, p, i: (b, i, p)),
                  pl.BlockSpec((None, S, LANES), lambda b, p, i: (b, 0, p)),
                  pl.BlockSpec((None, S, LANES), lambda b, p, i: (b, 0, pairs + p)),
                  pl.BlockSpec((TK, TK), lambda b, p, i: (0, 0))],
        out_specs=pl.BlockSpec((None, TQ, LANES), lambda b, p, i: (b, i, p)),
        out_shape=jax.ShapeDtypeStruct((B, S, SEQ_MIX_W), BF16),
        scratch_shapes=[pltpu.VMEM((2, TQ, LANES), F32), pltpu.VMEM((2, TQ, 1), F32)],
        compiler_params=_cparams(("parallel", "parallel", "arbitrary"), 32),
        name="stick_breaking",
    )(h3, kv3, kv3, later)


def _mem_kernel(q_ref, kv_ref, o_ref):
    q = q_ref[...] * (1.0 / math.sqrt(HEAD_DIM))
    outs = []
    for h in range(MEM_HEADS):
        lanes = slice(h * HEAD_DIM, (h + 1) * HEAD_DIM)
        vl = slice(MEM_W + h * HEAD_DIM, MEM_W + (h + 1) * HEAD_DIM)
        s = _dot_nt(q[:, lanes].astype(BF16), kv_ref[:, lanes].astype(BF16))
        e = jnp.exp(s - jnp.max(s, axis=-1, keepdims=True))
        den = jnp.sum(e, axis=-1, keepdims=True)
        outs.append(_dot(e.astype(BF16), kv_ref[:, vl].astype(BF16)) / den)
    o_ref[...] = jnp.concatenate(outs, axis=1).astype(o_ref.dtype)


def _mem_attention(h3, mem_kv3, q_col_block, tm):
    B, S, _ = h3.shape
    return pl.pallas_call(
        _mem_kernel,
        grid=(B, S // tm),
        in_specs=[pl.BlockSpec((None, tm, MEM_W), lambda b, i: (b, i, q_col_block)),
                  pl.BlockSpec((None, MEM_LEN, 2 * MEM_W), lambda b, i: (b, 0, 0))],
        out_specs=pl.BlockSpec((None, tm, MEM_W), lambda b, i: (b, i, 0)),
        out_shape=jax.ShapeDtypeStruct((B, S, MEM_W), BF16),
        compiler_params=_cparams(("parallel", "parallel"), 32),
        name="mem_attention",
    )(h3, mem_kv3)


def _route(lg):
    lane = lax.broadcasted_iota(jnp.int32, lg.shape, 1)
    lane_f = lane.astype(F32)
    neg = -jnp.inf
    far = float(LANES)

    lgg = jnp.where((lane >= N_EXPERTS) & (lane < N_EXPERTS + N_GROUPS), lg, neg)
    gmax = jnp.max(lgg, axis=-1, keepdims=True)
    p_top = 1.0 / jnp.sum(jnp.exp(lgg - gmax), axis=-1, keepdims=True)
    g_lane = jnp.min(jnp.where(lgg == gmax, lane_f, far), axis=-1, keepdims=True)
    first = (g_lane - float(N_EXPERTS)) * float(EXPERTS_PER_GROUP)

    le = jnp.where((lane_f >= first) & (lane_f < first + float(EXPERTS_PER_GROUP)), lg, neg)
    m1 = jnp.max(le, axis=-1, keepdims=True)
    i1 = jnp.min(jnp.where(le == m1, lane_f, far), axis=-1, keepdims=True)
    le2 = jnp.where(lane_f == i1, neg, le)
    m2 = jnp.max(le2, axis=-1, keepdims=True)
    i2 = jnp.min(jnp.where(le2 == m2, lane_f, far), axis=-1, keepdims=True)
    e2 = jnp.exp(m2 - m1)
    den = 1.0 + e2
    gates = p_top * (jnp.where(lane_f == i1, 1.0 / den, 0.0) + jnp.where(lane_f == i2, e2 / den, 0.0))
    return gates, g_lane - float(N_EXPERTS)


def _oproj_kernel(seq_ref, mem_ref, x_ref, wo_ref, g_ref, b_ref, wr_ref, br_ref, before_ref,
                  o_ref, gates_ref, drow_ref, cnt_ref, wob_ref, wrh_ref, wrl_ref):
    @pl.when(pl.program_id(0) == 0)
    def _():
        wob_ref[...] = wo_ref[...].astype(BF16)
        hi, lo = _split_bf16(wr_ref[...])
        wrh_ref[...] = hi
        wrl_ref[...] = lo

    y = (_dot(seq_ref[...], wob_ref[0:SEQ_MIX_W, :]) + _dot(mem_ref[...], wob_ref[SEQ_MIX_W:, :]))
    x1 = _layer_norm(DEEPNORM_ALPHA * x_ref[...] + y, g_ref[...], b_ref[...])
    o_ref[...] = x1
    xh, xl = _split_bf16(x1)
    wrh = wrh_ref[...]
    gates, grp = _route(_dot(xh, wrh) + _dot(xh, wrl_ref[...]) + _dot(xl, wrh) + br_ref[...])

    lane_f = lax.broadcasted_iota(jnp.int32, gates.shape, 1).astype(F32)
    onehot = jnp.where(lane_f == grp, 1.0, 0.0)
    ranks = _dot(before_ref[...], onehot.astype(BF16))
    rank = jnp.sum(jnp.where(lane_f == grp, ranks, 0.0), axis=-1, keepdims=True)
    counts = jnp.sum(onehot, axis=0, keepdims=True)
    offset = jnp.sum(jnp.where(lane_f < grp, counts, 0.0), axis=-1, keepdims=True)
    dest = offset + rank
    gates_ref[...] = gates + jnp.where(lane_f == float(MOE_DEST_LANE), dest, 0.0)
    cnt_ref[...] = jnp.broadcast_to(counts, cnt_ref.shape)
    hi16 = jnp.floor(dest * (1.0 / 16.0))
    digits = (jnp.where(lane_f == 0.0, hi16, 0.0) + jnp.where(lane_f == 1.0, dest - 16.0 * hi16, 0.0))
    sel_lane = lax.broadcasted_iota(jnp.int32, (8, LANES), 1)
    sel = jnp.where(sel_lane == 0, 16.0, jnp.where(sel_lane == 1, 1.0, 0.0)).astype(BF16)
    drow_ref[...] = _dot_nt(sel, digits.astype(BF16))


def _oproj_ln_route(seq, memo, x, w_o, layer, g, b, w_group, b_group, w_router, b_router):
    T = x.shape[0]
    tm = MOE_TM
    pad = LANES - N_EXPERTS - N_GROUPS
    wr = jnp.concatenate([w_router, w_group, jnp.zeros((D_MODEL, pad), F32)], axis=1)
    br = jnp.concatenate([b_router, b_group, jnp.zeros((pad,), F32)]).reshape(1, LANES)
    before = jnp.asarray(np.arange(tm)[None, :] < np.arange(tm)[:, None], dtype=BF16)
    const = lambda i: (0, 0)
    return pl.pallas_call(
        _oproj_kernel,
        grid=(T // tm,),
        in_specs=[pl.BlockSpec((tm, SEQ_MIX_W), lambda i: (i, 0)),
                  pl.BlockSpec((tm, MEM_W), lambda i: (i, 0)),
                  pl.BlockSpec((tm, D_MODEL), lambda i: (i, 0)),
                  pl.BlockSpec((None, D_MODEL, D_MODEL), lambda i: (layer, 0, 0)),
                  pl.BlockSpec((1, D_MODEL), const),
                  pl.BlockSpec((1, D_MODEL), const),
                  pl.BlockSpec((D_MODEL, LANES), const),
                  pl.BlockSpec((1, LANES), const),
                  pl.BlockSpec((tm, tm), const)],
        out_specs=[pl.BlockSpec((tm, D_MODEL), lambda i: (i, 0)),
                   pl.BlockSpec((tm, LANES), lambda i: (i, 0)),
                   pl.BlockSpec((8, tm), lambda i: (i, 0)),
                   pl.BlockSpec((8, LANES), lambda i: (i, 0))],
        out_shape=[jax.ShapeDtypeStruct((T, D_MODEL), F32),
                   jax.ShapeDtypeStruct((T, LANES), F32),
                   jax.ShapeDtypeStruct((8 * (T // tm), tm), F32),
                   jax.ShapeDtypeStruct((8 * (T // tm), LANES), F32)],
        scratch_shapes=[pltpu.VMEM((D_MODEL, D_MODEL), BF16),
                        pltpu.VMEM((D_MODEL, LANES), BF16),
                        pltpu.VMEM((D_MODEL, LANES), BF16)],
        compiler_params=_cparams(("arbitrary",), 40),
        name="oproj_ln_route",
    )(seq, memo, x, w_o, g.reshape(1, D_MODEL), b.reshape(1, D_MODEL), wr, br, before)


def _moe_kernel(npairs_ref, pblk_ref, pgrp_ref,
                x_ref, gates_ref, drow_ref, wg_ref, wu_ref, wd_ref, g_ref, b_ref, o_ref,
                wgu_s, wd_s, xs_ref, gs_ref, ys_ref):
    step = pl.program_id(0)
    tm = MOE_TM
    R = MOE_R

    @pl.when(step < N_EXPERTS)
    def _():
        wgu_s[step, :, 0:D_EXPERT] = wg_ref[...].astype(BF16)
        wgu_s[step, :, D_EXPERT:] = wu_ref[...].astype(BF16)
        row0 = pl.multiple_of((step % EXPERTS_PER_GROUP) * D_EXPERT, D_EXPERT)
        wd_s[step // EXPERTS_PER_GROUP, pl.ds(row0, D_EXPERT), :] = wd_ref[...].astype(BF16)

    @pl.when(step >= N_EXPERTS)
    def _():
        tile = step - N_EXPERTS
        x = x_ref[...]
        gates = gates_ref[...]
        lane_g = lax.broadcasted_iota(jnp.int32, gates.shape, 1)
        dest_col = jnp.sum(jnp.where(lane_g == MOE_DEST_LANE, gates, 0.0), axis=-1, keepdims=True)
        sorted_row = lax.broadcasted_iota(jnp.int32, (tm, tm), 0).astype(F32)
        perm = jnp.where(sorted_row == drow_ref[0:1, :], 1.0, 0.0).astype(BF16)
        xs_ref[...] = _dot(perm, x.astype(BF16)).astype(BF16)
        ghi, glo = _split_bf16(gates)
        g2 = _dot(perm, jnp.concatenate([ghi, glo], axis=1))
        gs_ref[...] = g2[:, :LANES] + g2[:, LANES:]
        ys_ref[...] = jnp.zeros_like(ys_ref)

        def pair(p, _):
            blk = pblk_ref[tile * MOE_MAX_PAIRS + p]
            grp = pgrp_ref[tile * MOE_MAX_PAIRS + p]
            rows = pl.ds(pl.multiple_of(blk * R, R), R)
            xblk = xs_ref[rows, :]
            gblk = gs_ref[rows, :]
            lane = lax.broadcasted_iota(jnp.int32, gblk.shape, 1)
            hs = []
            for j in range(EXPERTS_PER_GROUP):
                e = grp * EXPERTS_PER_GROUP + j
                hgu = _dot(xblk, wgu_s[e])
                hg = hgu[:, :D_EXPERT]
                ge = jnp.sum(jnp.where(lane == e, gblk, 0.0), axis=-1, keepdims=True)
                hs.append((hg * jax.nn.sigmoid(hg) * hgu[:, D_EXPERT:] * ge).astype(BF16))
            ys_ref[rows, :] += _dot(jnp.concatenate(hs, axis=1), wd_s[grp])
            return 0

        lax.fori_loop(0, npairs_ref[tile], pair, 0)

        sorted_col = lax.broadcasted_iota(jnp.int32, (tm, tm), 1).astype(F32)
        unperm = jnp.where(sorted_col == dest_col, 1.0, 0.0).astype(BF16)
        y = _dot(unperm, ys_ref[...].astype(BF16))
        o_ref[...] = _layer_norm(DEEPNORM_ALPHA * x + y, g_ref[...], b_ref[...])


def _moe_plan(counts):
    ntiles = counts.shape[0]
    nblk = MOE_TM // MOE_R
    end = jnp.cumsum(counts, axis=1)
    start = end - counts
    lo = (jnp.arange(nblk, dtype=jnp.int32) * MOE_R)[None, :, None]
    active = (counts[:, None, :] > 0) & (start[:, None, :] < lo + MOE_R) & (end[:, None, :] > lo)
    flat = active.reshape(ntiles, nblk * N_GROUPS)
    order = jnp.argsort(jnp.logical_not(flat), axis=1, stable=True)[:, :MOE_MAX_PAIRS].astype(jnp.int32)
    npairs = jnp.sum(flat, axis=1).astype(jnp.int32)
    return npairs, (order // N_GROUPS).reshape(-1), (order % N_GROUPS).reshape(-1)


def _moe_ln(x, gates, dest_rows, counts8, w_gate, w_up, w_down, layer, g, b):
    T = x.shape[0]
    tm = MOE_TM
    ntiles = T // tm
    counts = counts8.reshape(ntiles, 8, LANES)[:, 0, :N_GROUPS].astype(jnp.int32)
    npairs, pblk, pgrp = _moe_plan(counts)

    def tile_map(s, *_):
        return (jnp.maximum(s - N_EXPERTS, 0), 0)

    def w_map(s, *_):
        return (layer, jnp.minimum(s, N_EXPERTS - 1), 0, 0)

    const = lambda s, *_: (0, 0)
    grid_spec = pltpu.PrefetchScalarGridSpec(
        num_scalar_prefetch=3,
        grid=(N_EXPERTS + ntiles,),
        in_specs=[pl.BlockSpec((tm, D_MODEL), tile_map),
                  pl.BlockSpec((tm, LANES), tile_map),
                  pl.BlockSpec((8, tm), tile_map),
                  pl.BlockSpec((None, None, D_MODEL, D_EXPERT), w_map),
                  pl.BlockSpec((None, None, D_MODEL, D_EXPERT), w_map),
                  pl.BlockSpec((None, None, D_EXPERT, D_MODEL), w_map),
                  pl.BlockSpec((1, D_MODEL), const),
                  pl.BlockSpec((1, D_MODEL), const)],
        out_specs=pl.BlockSpec((tm, D_MODEL), tile_map),
        scratch_shapes=[pltpu.VMEM((N_EXPERTS, D_MODEL, 2 * D_EXPERT), BF16),
                        pltpu.VMEM((N_GROUPS, EXPERTS_PER_GROUP * D_EXPERT, D_MODEL), BF16),
                        pltpu.VMEM((tm, D_MODEL), BF16),
                        pltpu.VMEM((tm, LANES), F32),
                        pltpu.VMEM((tm, D_MODEL), F32)])
    return pl.pallas_call(
        _moe_kernel,
        grid_spec=grid_spec,
        out_shape=jax.ShapeDtypeStruct((T, D_MODEL), F32),
        compiler_params=_cparams(("arbitrary",), 56),
        name="moe_ln",
    )(npairs, pblk, pgrp, x, gates, dest_rows, w_gate, w_up, w_down,
      g.reshape(1, D_MODEL), b.reshape(1, D_MODEL))


def kernel(x, mem, a_w_in, a_lower_bounds, a_gnorm, b_w_in, w_kv_shared, w_mem_kv, w_o,
           ln_mix_g, ln_mix_b, ln_ffn_g, ln_ffn_b, w_group, b_group, w_router, b_router,
           w_gate, w_up, w_down):
    B, S, D = x.shape
    T = B * S
    xf = x.reshape(T, D)
    memf = mem.reshape(B * MEM_LEN, D)
    a_cols = a_w_in.shape[-1]
    b_cols = b_w_in.shape[-1]
    kv3 = None
    for layer in range(DEPTH):
        mem_kv = _matmul(memf, w_mem_kv, layer, 512, 2 * MEM_W, "mem_kv")
        mem_kv3 = mem_kv.reshape(B, MEM_LEN, 2 * MEM_W)
        if layer < N_A_LAYERS:
            h = _matmul(xf, a_w_in, layer, 512, a_cols // 2, "in_proj_a")
            h3 = h.reshape(B, S, a_cols)
            seq = _hgrn(h3, a_lower_bounds, a_gnorm[layer], layer)
            q_col_block = 4 * SEQ_MIX_W // MEM_W
        else:
            if layer == N_A_LAYERS:
                kv = _matmul(xf, w_kv_shared[None], 0, 512, SEQ_MIX_W, "kv_shared", BF16)
                kv3 = kv.reshape(B, S, 2 * SEQ_MIX_W)
            h = _matmul(xf, b_w_in, layer - N_A_LAYERS, 512, b_cols, "in_proj_b", BF16)
            h3 = h.reshape(B, S, b_cols)
            seq = _sb_attention(h3, kv3)
            q_col_block = SEQ_MIX_W // MEM_W
        memo = _mem_attention(h3, mem_kv3, q_col_block, 512)
        x1, gates, dest_rows, counts8 = _oproj_ln_route(
            seq.reshape(T, SEQ_MIX_W), memo.reshape(T, MEM_W), xf, w_o, layer,
            ln_mix_g[layer], ln_mix_b[layer],
            w_group[layer], b_group[layer], w_router[layer], b_router[layer])
        xf = _moe_ln(x1, gates, dest_rows, counts8, w_gate, w_up, w_down, layer,
                     ln_ffn_g[layer], ln_ffn_b[layer])
    return xf.reshape(B, S, D)
```

```python
import functools
import math

import numpy as np
import jax
import jax.numpy as jnp
from jax import lax
from jax.experimental import pallas as pl
from jax.experimental.pallas import tpu as pltpu

F32 = jnp.float32
BF16 = jnp.bfloat16

D_MODEL = 1024
DEPTH = 4
N_A_LAYERS = DEPTH // 2
MEM_LEN = 256
MEM_HEADS = 4
HEAD_DIM = 64
MEM_W = MEM_HEADS * HEAD_DIM
SEQ_MIX_W = D_MODEL - MEM_W
HGRN_DK = 128
HGRN_HEADS = SEQ_MIX_W // HGRN_DK
SB_HEADS = SEQ_MIX_W // HEAD_DIM
N_GROUPS = 4
EXPERTS_PER_GROUP = 4
N_EXPERTS = N_GROUPS * EXPERTS_PER_GROUP
D_EXPERT = 256
DEEPNORM_ALPHA = (2 * DEPTH) ** 0.25
LN_EPS = 1e-5
RMS_EPS = 1e-6

LANES = 128
HGRN_C = 128
SB_TK = 256
SB_ZERO_MASS = 110.0
MOE_TM = 512
MOE_R = 128
MOE_MAX_PAIRS = MOE_TM // MOE_R + N_GROUPS - 1
MOE_DEST_LANE = N_EXPERTS
MIB = 1024 * 1024


def _cparams(semantics, vmem_mib):
    return pltpu.CompilerParams(dimension_semantics=semantics, vmem_limit_bytes=vmem_mib * MIB)


def _dot(a, b):
    return jnp.dot(a, b, preferred_element_type=F32)


def _dot_nt(a, b):
    return lax.dot_general(a, b, (((1,), (1,)), ((), ())), preferred_element_type=F32)


def _dot_tn(a, b):
    return lax.dot_general(a, b, (((0,), (0,)), ((), ())), preferred_element_type=F32)


def _split_bf16(x):
    hi = x.astype(BF16)
    lo = (x - hi.astype(F32)).astype(BF16)
    return hi, lo


def _layer_norm(r, g, b):
    mu = jnp.mean(r, axis=-1, keepdims=True)
    d = r - mu
    var = jnp.mean(d * d, axis=-1, keepdims=True)
    return d * lax.rsqrt(var + LN_EPS) * g + b


def _mm_kernel(x_ref, w_ref, o_ref, wb_ref):
    @pl.when(pl.program_id(1) == 0)
    def _():
        wb_ref[...] = w_ref[...].astype(BF16)

    o_ref[...] = _dot(x_ref[...].astype(BF16), wb_ref[...]).astype(o_ref.dtype)


def _matmul(x, w_stack, layer, tm, tn, name, out_dtype=F32):
    M, K = x.shape
    N = w_stack.shape[2]
    assert M % tm == 0 and N % tn == 0
    return pl.pallas_call(
        _mm_kernel,
        grid=(N // tn, M // tm),
        in_specs=[pl.BlockSpec((tm, K), lambda j, i: (i, 0)),
                  pl.BlockSpec((None, K, tn), lambda j, i: (layer, 0, j))],
        out_specs=pl.BlockSpec((tm, tn), lambda j, i: (i, j)),
        out_shape=jax.ShapeDtypeStruct((M, N), out_dtype),
        scratch_shapes=[pltpu.VMEM((K, tn), BF16)],
        compiler_params=_cparams(("parallel", "arbitrary"), 48),
        name=name,
    )(x, w_stack)


def _hgrn_levels():
    m = HGRN_C // 2
    out = []
    while m >= 1:
        out.append(m)
        m //= 2
    return out


def _hgrn_sum_matrix():
    C = HGRN_C
    r = np.arange(C)[:, None]
    c = np.arange(C)[None, :]
    blocks = [(c <= r), (c > r)]
    for m in _hgrn_levels():
        p = r % (2 * m)
        mid = r - p + m - 1
        upper = (p >= m) & (c > mid) & (c <= r)
        lower = (p < m) & (c > r) & (c <= mid)
        blocks.append(upper | lower)
    return np.concatenate(blocks, axis=0).astype(np.float32)


def _hgrn_kernel(layer, q_ref, f_ref, i_ref, g_ref, lbp_ref, gn_ref, sum_ref, o_ref, st_ref):
    C = HGRN_C

    @pl.when(pl.program_id(1) == 0)
    def _():
        st_ref[...] = jnp.zeros_like(st_ref)

    a = lbp_ref[...]
    ea = jnp.exp(a - jnp.max(a, axis=0, keepdims=True))
    pa = ea / jnp.sum(ea, axis=0, keepdims=True)
    lb_all = jnp.zeros((1, SEQ_MIX_W), F32)
    for l in range(1, layer + 1):
        lb_all = lb_all + pa[l:l + 1]

    t_idx = lax.broadcasted_iota(jnp.int32, (C, C), 0)
    s_idx = lax.broadcasted_iota(jnp.int32, (C, C), 1)
    split = t_idx ^ s_idx
    levels = _hgrn_levels()
    sum_mat = sum_ref[...]

    for h in range(HGRN_HEADS):
        lanes = slice(h * LANES, (h + 1) * LANES)
        lb = lb_all[:, lanes]
        fr = f_ref[:, lanes]
        logsig = jnp.minimum(fr, 0.0) - jnp.log1p(jnp.exp(-jnp.abs(fr)))
        la = jnp.log(lb)
        lc = jnp.log1p(-lb) + logsig
        logf = jnp.maximum(la, lc) + jnp.log1p(jnp.exp(-jnp.abs(la - lc)))
        k = 1.0 - jnp.exp(logf)

        hi, lo = _split_bf16(logf)
        e2 = _dot(sum_mat, jnp.concatenate([hi, lo], axis=1))
        dec = jnp.exp(e2[:, :LANES] + e2[:, LANES:])
        dec_g = dec[0:C]
        dec_suf = dec[C:2 * C]

        qr = q_ref[:, lanes]
        q = qr * jax.nn.sigmoid(qr) * (HGRN_DK ** -0.5)
        v = i_ref[:, lanes].astype(BF16)

        sc = None
        for li in range(len(levels) - 1, -1, -1):
            d = dec[(2 + li) * C:(3 + li) * C]
            p = _dot_nt((q * d).astype(BF16), (k * d).astype(BF16))
            sc = p if sc is None else jnp.where(split >= levels[li], p, sc)
        sc = jnp.where(s_idx < t_idx, sc, 0.0)
        sc = jnp.where(s_idx == t_idx, jnp.sum(q * k, axis=-1, keepdims=True), sc)

        st = st_ref[h]
        o = _dot(sc.astype(BF16), v) + _dot_nt((q * dec_g).astype(BF16), st.astype(BF16))
        st_ref[h] = st * dec_g[C - 1:C, :] + _dot_tn(v, (k * dec_suf).astype(BF16))

        o = o * lax.rsqrt(jnp.mean(o * o, axis=-1, keepdims=True) + RMS_EPS)
        gr = g_ref[:, lanes]
        o_ref[:, lanes] = (o * gn_ref[:, lanes] * (gr * jax.nn.sigmoid(gr))).astype(o_ref.dtype)


def _hgrn(h3, a_lower_bounds, gnorm, layer):
    B, S, _ = h3.shape
    C = HGRN_C
    W = SEQ_MIX_W
    sum_mat = jnp.asarray(_hgrn_sum_matrix(), dtype=BF16)

    def col(j):
        return pl.BlockSpec((None, C, W), lambda b, c: (b, c, j))

    return pl.pallas_call(
        functools.partial(_hgrn_kernel, layer),
        grid=(B, S // C),
        in_specs=[col(0), col(1), col(2), col(3),
                  pl.BlockSpec((N_A_LAYERS, W), lambda b, c: (0, 0)),
                  pl.BlockSpec((1, W), lambda b, c: (0, 0)),
                  pl.BlockSpec(sum_mat.shape, lambda b, c: (0, 0))],
        out_specs=pl.BlockSpec((None, C, W), lambda b, c: (b, c, 0)),
        out_shape=jax.ShapeDtypeStruct((B, S, W), BF16),
        scratch_shapes=[pltpu.VMEM((HGRN_HEADS, LANES, HGRN_DK), F32)],
        compiler_params=_cparams(("parallel", "arbitrary"), 32),
        name="hgrn2",
    )(h3, h3, h3, h3, a_lower_bounds, gnorm.reshape(1, W), sum_mat)


def _sb_kernel(q_ref, k_ref, v_ref, later_ref, o_ref, acc_ref, carry_ref):
    TK = SB_TK
    TQ = 2 * TK
    qi = pl.program_id(2)
    lane = lax.broadcasted_iota(jnp.int32, (TQ, LANES), 1)
    q = q_ref[...] * (1.0 / math.sqrt(HEAD_DIM))
    zero = jnp.zeros_like(q)
    qh = [jnp.where(lane < HEAD_DIM, q, zero), jnp.where(lane >= HEAD_DIM, q, zero)]
    later = later_ref[...]
    row = lax.broadcasted_iota(jnp.int32, (TK, TK), 0)
    col = lax.broadcasted_iota(jnp.int32, (TK, TK), 1)
    causal = col < row

    def sweep(h, rows, kb, masked, carry):
        start = pl.multiple_of(kb * TK, TK)
        z = _dot_nt(qh[h][rows], k_ref[pl.ds(start, TK), :])
        sp = jnp.maximum(z, 0.0) + jnp.log(1.0 + jnp.exp(-jnp.abs(z)))
        spm = jnp.where(causal, sp, 0.0) if masked else sp
        rest = _dot(spm.astype(BF16), later) + carry
        a = jnp.exp(z - sp - rest)
        if masked:
            a = jnp.where(causal, a, 0.0)
        contrib = _dot(a.astype(BF16), v_ref[pl.ds(start, TK), :])
        return contrib, carry + jnp.sum(spm, axis=1, keepdims=True)

    top = slice(0, TK)
    bot = slice(TK, TQ)
    no_mass = jnp.zeros((TK, 1), F32)
    has_prev = qi > 0
    for h in range(2):
        acc_t, c_t = sweep(h, top, 2 * qi, True, no_mass)
        acc_t2, c_t2 = sweep(h, top, jnp.maximum(2 * qi - 1, 0), False, c_t)
        acc_b, c_b = sweep(h, bot, 2 * qi + 1, True, no_mass)
        acc_b2, c_b = sweep(h, bot, 2 * qi, False, c_b)
        acc_ref[h, top, :] = acc_t + jnp.where(has_prev, acc_t2, 0.0)
        acc_ref[h, bot, :] = acc_b + acc_b2
        carry_ref[h, top, :] = jnp.where(has_prev, c_t2, c_t)
        carry_ref[h, bot, :] = c_b

    def min_mass():
        return jnp.minimum(jnp.min(carry_ref[0]), jnp.min(carry_ref[1]))

    def more(state):
        j, mass = state
        return jnp.logical_and(j < 2 * qi, mass < SB_ZERO_MASS)

    def earlier(state):
        j, _ = state
        kb_bot = 2 * qi - 1 - j
        kb_top = kb_bot - 1
        top_live = kb_top >= 0
        for h in range(2):
            contrib, c = sweep(h, top, jnp.maximum(kb_top, 0), False, carry_ref[h, top, :])
            acc_ref[h, top, :] += jnp.where(top_live, contrib, 0.0)
            carry_ref[h, top, :] = jnp.where(top_live, c, carry_ref[h, top, :])
            contrib, c = sweep(h, bot, kb_bot, False, carry_ref[h, bot, :])
            acc_ref[h, bot, :] += contrib
            carry_ref[h, bot, :] = c
        return j + 1, min_mass()

    lax.while_loop(more, earlier, (jnp.int32(0), min_mass()))
    o_ref[...] = jnp.where(lane < HEAD_DIM, acc_ref[0], acc_ref[1]).astype(o_ref.dtype)


def _sb_attention(h3, kv3):
    B, S, _ = h3.shape
    TK = SB_TK
    TQ = 2 * TK
    pairs = SEQ_MIX_W // LANES
    later = jnp.asarray(np.arange(TK)[:, None] > np.arange(TK)[None, :], dtype=BF16)
    return pl.pallas_call(
        _sb_kernel,
        grid=(B, pairs, S // TQ),
        in_specs=[pl.BlockSpec((None, TQ, LANES), lambda b, p, i: (b, i, p)),
                  pl.BlockSpec((None, S, LANES), lambda b, p, i: (b, 0, p)),
                  pl.BlockSpec((None, S, LANES), lambda b, p, i: (b, 0, pairs + p)),
                  pl.BlockSpec((TK, TK), lambda b, p, i: (0, 0))],
        out_specs=pl.BlockSpec((None, TQ, LANES), lambda b, p, i: (b, i, p)),
        out_shape=jax.ShapeDtypeStruct((B, S, SEQ_MIX_W), BF16),
        scratch_shapes=[pltpu.VMEM((2, TQ, LANES), F32), pltpu.VMEM((2, TQ, 1), F32)],
        compiler_params=_cparams(("parallel", "parallel", "arbitrary"), 32),
        name="stick_breaking",
    )(h3, kv3, kv3, later)


def _mem_kernel(q_ref, kv_ref, o_ref):
    q = q_ref[...] * (1.0 / math.sqrt(HEAD_DIM))
    outs = []
    for h in range(MEM_HEADS):
        lanes = slice(h * HEAD_DIM, (h + 1) * HEAD_DIM)
        vl = slice(MEM_W + h * HEAD_DIM, MEM_W + (h + 1) * HEAD_DIM)
        s = _dot_nt(q[:, lanes].astype(BF16), kv_ref[:, lanes].astype(BF16))
        e = jnp.exp(s - jnp.max(s, axis=-1, keepdims=True))
        den = jnp.sum(e, axis=-1, keepdims=True)
        outs.append(_dot(e.astype(BF16), kv_ref[:, vl].astype(BF16)) / den)
    o_ref[...] = jnp.concatenate(outs, axis=1).astype(o_ref.dtype)


def _mem_attention(h3, mem_kv3, q_col_block, tm):
    B, S, _ = h3.shape
    return pl.pallas_call(
        _mem_kernel,
        grid=(B, S // tm),
        in_specs=[pl.BlockSpec((None, tm, MEM_W), lambda b, i: (b, i, q_col_block)),
                  pl.BlockSpec((None, MEM_LEN, 2 * MEM_W), lambda b, i: (b, 0, 0))],
        out_specs=pl.BlockSpec((None, tm, MEM_W), lambda b, i: (b, i, 0)),
        out_shape=jax.ShapeDtypeStruct((B, S, MEM_W), BF16),
        compiler_params=_cparams(("parallel", "parallel"), 32),
        name="mem_attention",
    )(h3, mem_kv3)


def _route(lg):
    lane = lax.broadcasted_iota(jnp.int32, lg.shape, 1)
    lane_f = lane.astype(F32)
    neg = -jnp.inf
    far = float(LANES)

    lgg = jnp.where((lane >= N_EXPERTS) & (lane < N_EXPERTS + N_GROUPS), lg, neg)
    gmax = jnp.max(lgg, axis=-1, keepdims=True)
    p_top = 1.0 / jnp.sum(jnp.exp(lgg - gmax), axis=-1, keepdims=True)
    g_lane = jnp.min(jnp.where(lgg == gmax, lane_f, far), axis=-1, keepdims=True)
    first = (g_lane - float(N_EXPERTS)) * float(EXPERTS_PER_GROUP)

    le = jnp.where((lane_f >= first) & (lane_f < first + float(EXPERTS_PER_GROUP)), lg, neg)
    m1 = jnp.max(le, axis=-1, keepdims=True)
    i1 = jnp.min(jnp.where(le == m1, lane_f, far), axis=-1, keepdims=True)
    le2 = jnp.where(lane_f == i1, neg, le)
    m2 = jnp.max(le2, axis=-1, keepdims=True)
    i2 = jnp.min(jnp.where(le2 == m2, lane_f, far), axis=-1, keepdims=True)
    e2 = jnp.exp(m2 - m1)
    den = 1.0 + e2
    gates = p_top * (jnp.where(lane_f == i1, 1.0 / den, 0.0) + jnp.where(lane_f == i2, e2 / den, 0.0))
    return gates, g_lane - float(N_EXPERTS)


def _oproj_kernel(seq_ref, mem_ref, x_ref, wo_ref, g_ref, b_ref, wr_ref, br_ref, before_ref,
                  o_ref, gates_ref, drow_ref, cnt_ref, wob_ref, wrh_ref, wrl_ref):
    @pl.when(pl.program_id(0) == 0)
    def _():
        wob_ref[...] = wo_ref[...].astype(BF16)
        hi, lo = _split_bf16(wr_ref[...])
        wrh_ref[...] = hi
        wrl_ref[...] = lo

    y = (_dot(seq_ref[...], wob_ref[0:SEQ_MIX_W, :]) + _dot(mem_ref[...], wob_ref[SEQ_MIX_W:, :]))
    x1 = _layer_norm(DEEPNORM_ALPHA * x_ref[...] + y, g_ref[...], b_ref[...])
    o_ref[...] = x1
    xh, xl = _split_bf16(x1)
    wrh = wrh_ref[...]
    gates, grp = _route(_dot(xh, wrh) + _dot(xh, wrl_ref[...]) + _dot(xl, wrh) + br_ref[...])

    lane_f = lax.broadcasted_iota(jnp.int32, gates.shape, 1).astype(F32)
    onehot = jnp.where(lane_f == grp, 1.0, 0.0)
    ranks = _dot(before_ref[...], onehot.astype(BF16))
    rank = jnp.sum(jnp.where(lane_f == grp, ranks, 0.0), axis=-1, keepdims=True)
    counts = jnp.sum(onehot, axis=0, keepdims=True)
    offset = jnp.sum(jnp.where(lane_f < grp, counts, 0.0), axis=-1, keepdims=True)
    dest = offset + rank
    gates_ref[...] = gates + jnp.where(lane_f == float(MOE_DEST_LANE), dest, 0.0)
    cnt_ref[...] = jnp.broadcast_to(counts, cnt_ref.shape)
    hi16 = jnp.floor(dest * (1.0 / 16.0))
    digits = (jnp.where(lane_f == 0.0, hi16, 0.0) + jnp.where(lane_f == 1.0, dest - 16.0 * hi16, 0.0))
    sel_lane = lax.broadcasted_iota(jnp.int32, (8, LANES), 1)
    sel = jnp.where(sel_lane == 0, 16.0, jnp.where(sel_lane == 1, 1.0, 0.0)).astype(BF16)
    drow_ref[...] = _dot_nt(sel, digits.astype(BF16))


def _oproj_ln_route(seq, memo, x, w_o, layer, g, b, w_group, b_group, w_router, b_router):
    T = x.shape[0]
    tm = MOE_TM
    pad = LANES - N_EXPERTS - N_GROUPS
    wr = jnp.concatenate([w_router, w_group, jnp.zeros((D_MODEL, pad), F32)], axis=1)
    br = jnp.concatenate([b_router, b_group, jnp.zeros((pad,), F32)]).reshape(1, LANES)
    before = jnp.asarray(np.arange(tm)[None, :] < np.arange(tm)[:, None], dtype=BF16)
    const = lambda i: (0, 0)
    return pl.pallas_call(
        _oproj_kernel,
        grid=(T // tm,),
        in_specs=[pl.BlockSpec((tm, SEQ_MIX_W), lambda i: (i, 0)),
                  pl.BlockSpec((tm, MEM_W), lambda i: (i, 0)),
                  pl.BlockSpec((tm, D_MODEL), lambda i: (i, 0)),
                  pl.BlockSpec((None, D_MODEL, D_MODEL), lambda i: (layer, 0, 0)),
                  pl.BlockSpec((1, D_MODEL), const),
                  pl.BlockSpec((1, D_MODEL), const),
                  pl.BlockSpec((D_MODEL, LANES), const),
                  pl.BlockSpec((1, LANES), const),
                  pl.BlockSpec((tm, tm), const)],
        out_specs=[pl.BlockSpec((tm, D_MODEL), lambda i: (i, 0)),
                   pl.BlockSpec((tm, LANES), lambda i: (i, 0)),
                   pl.BlockSpec((8, tm), lambda i: (i, 0)),
                   pl.BlockSpec((8, LANES), lambda i: (i, 0))],
        out_shape=[jax.ShapeDtypeStruct((T, D_MODEL), F32),
                   jax.ShapeDtypeStruct((T, LANES), F32),
                   jax.ShapeDtypeStruct((8 * (T // tm), tm), F32),
                   jax.ShapeDtypeStruct((8 * (T // tm), LANES), F32)],
        scratch_shapes=[pltpu.VMEM((D_MODEL, D_MODEL), BF16),
                        pltpu.VMEM((D_MODEL, LANES), BF16),
                        pltpu.VMEM((D_MODEL, LANES), BF16)],
        compiler_params=_cparams(("arbitrary",), 40),
        name="oproj_ln_route",
    )(seq, memo, x, w_o, g.reshape(1, D_MODEL), b.reshape(1, D_MODEL), wr, br, before)


def _moe_kernel(npairs_ref, pblk_ref, pgrp_ref,
                x_ref, gates_ref, drow_ref, wg_ref, wu_ref, wd_ref, g_ref, b_ref, o_ref,
                wgu_s, wd_s, xs_ref, gs_ref, ys_ref):
    step = pl.program_id(0)
    tm = MOE_TM
    R = MOE_R

    @pl.when(step < N_EXPERTS)
    def _():
        wgu_s[step, :, 0:D_EXPERT] = wg_ref[...].astype(BF16)
        wgu_s[step, :, D_EXPERT:] = wu_ref[...].astype(BF16)
        row0 = pl.multiple_of((step % EXPERTS_PER_GROUP) * D_EXPERT, D_EXPERT)
        wd_s[step // EXPERTS_PER_GROUP, pl.ds(row0, D_EXPERT), :] = wd_ref[...].astype(BF16)

    @pl.when(step >= N_EXPERTS)
    def _():
        tile = step - N_EXPERTS
        x = x_ref[...]
        gates = gates_ref[...]
        lane_g = lax.broadcasted_iota(jnp.int32, gates.shape, 1)
        dest_col = jnp.sum(jnp.where(lane_g == MOE_DEST_LANE, gates, 0.0), axis=-1, keepdims=True)
        sorted_row = lax.broadcasted_iota(jnp.int32, (tm, tm), 0).astype(F32)
        perm = jnp.where(sorted_row == drow_ref[0:1, :], 1.0, 0.0).astype(BF16)
        xs_ref[...] = _dot(perm, x.astype(BF16)).astype(BF16)
        ghi, glo = _split_bf16(gates)
        g2 = _dot(perm, jnp.concatenate([ghi, glo], axis=1))
        gs_ref[...] = g2[:, :LANES] + g2[:, LANES:]
        ys_ref[...] = jnp.zeros_like(ys_ref)

        def pair(p, _):
            blk = pblk_ref[tile * MOE_MAX_PAIRS + p]
            grp = pgrp_ref[tile * MOE_MAX_PAIRS + p]
            rows = pl.ds(pl.multiple_of(blk * R, R), R)
            xblk = xs_ref[rows, :]
            gblk = gs_ref[rows, :]
            lane = lax.broadcasted_iota(jnp.int32, gblk.shape, 1)
            hs = []
            for j in range(EXPERTS_PER_GROUP):
                e = grp * EXPERTS_PER_GROUP + j
                hgu = _dot(xblk, wgu_s[e])
                hg = hgu[:, :D_EXPERT]
                ge = jnp.sum(jnp.where(lane == e, gblk, 0.0), axis=-1, keepdims=True)
                hs.append((hg * jax.nn.sigmoid(hg) * hgu[:, D_EXPERT:] * ge).astype(BF16))
            ys_ref[rows, :] += _dot(jnp.concatenate(hs, axis=1), wd_s[grp])
            return 0

        lax.fori_loop(0, npairs_ref[tile], pair, 0)

        sorted_col = lax.broadcasted_iota(jnp.int32, (tm, tm), 1).astype(F32)
        unperm = jnp.where(sorted_col == dest_col, 1.0, 0.0).astype(BF16)
        y = _dot(unperm, ys_ref[...].astype(BF16))
        o_ref[...] = _layer_norm(DEEPNORM_ALPHA * x + y, g_ref[...], b_ref[...])


def _moe_plan(counts):
    ntiles = counts.shape[0]
    nblk = MOE_TM // MOE_R
    end = jnp.cumsum(counts, axis=1)
    start = end - counts
    lo = (jnp.arange(nblk, dtype=jnp.int32) * MOE_R)[None, :, None]
    active = (counts[:, None, :] > 0) & (start[:, None, :] < lo + MOE_R) & (end[:, None, :] > lo)
    flat = active.reshape(ntiles, nblk * N_GROUPS)
    order = jnp.argsort(jnp.logical_not(flat), axis=1, stable=True)[:, :MOE_MAX_PAIRS].astype(jnp.int32)
    npairs = jnp.sum(flat, axis=1).astype(jnp.int32)
    return npairs, (order // N_GROUPS).reshape(-1), (order % N_GROUPS).reshape(-1)


def _moe_ln(x, gates, dest_rows, counts8, w_gate, w_up, w_down, layer, g, b):
    T = x.shape[0]
    tm = MOE_TM
    ntiles = T // tm
    counts = counts8.reshape(ntiles, 8, LANES)[:, 0, :N_GROUPS].astype(jnp.int32)
    npairs, pblk, pgrp = _moe_plan(counts)

    def tile_map(s, *_):
        return (jnp.maximum(s - N_EXPERTS, 0), 0)

    def w_map(s, *_):
        return (layer, jnp.minimum(s, N_EXPERTS - 1), 0, 0)

    const = lambda s, *_: (0, 0)
    grid_spec = pltpu.PrefetchScalarGridSpec(
        num_scalar_prefetch=3,
        grid=(N_EXPERTS + ntiles,),
        in_specs=[pl.BlockSpec((tm, D_MODEL), tile_map),
                  pl.BlockSpec((tm, LANES), tile_map),
                  pl.BlockSpec((8, tm), tile_map),
                  pl.BlockSpec((None, None, D_MODEL, D_EXPERT), w_map),
                  pl.BlockSpec((None, None, D_MODEL, D_EXPERT), w_map),
                  pl.BlockSpec((None, None, D_EXPERT, D_MODEL), w_map),
                  pl.BlockSpec((1, D_MODEL), const),
                  pl.BlockSpec((1, D_MODEL), const)],
        out_specs=pl.BlockSpec((tm, D_MODEL), tile_map),
        scratch_shapes=[pltpu.VMEM((N_EXPERTS, D_MODEL, 2 * D_EXPERT), BF16),
                        pltpu.VMEM((N_GROUPS, EXPERTS_PER_GROUP * D_EXPERT, D_MODEL), BF16),
                        pltpu.VMEM((tm, D_MODEL), BF16),
                        pltpu.VMEM((tm, LANES), F32),
                        pltpu.VMEM((tm, D_MODEL), F32)])
    return pl.pallas_call(
        _moe_kernel,
        grid_spec=grid_spec,
        out_shape=jax.ShapeDtypeStruct((T, D_MODEL), F32),
        compiler_params=_cparams(("arbitrary",), 56),
        name="moe_ln",
    )(npairs, pblk, pgrp, x, gates, dest_rows, w_gate, w_up, w_down,
      g.reshape(1, D_MODEL), b.reshape(1, D_MODEL))


def kernel(x, mem, a_w_in, a_lower_bounds, a_gnorm, b_w_in, w_kv_shared, w_mem_kv, w_o,
           ln_mix_g, ln_mix_b, ln_ffn_g, ln_ffn_b, w_group, b_group, w_router, b_router,
           w_gate, w_up, w_down):
    B, S, D = x.shape
    T = B * S
    xf = x.reshape(T, D)
    memf = mem.reshape(B * MEM_LEN, D)
    a_cols = a_w_in.shape[-1]
    b_cols = b_w_in.shape[-1]
    kv3 = None
    for layer in range(DEPTH):
        mem_kv = _matmul(memf, w_mem_kv, layer, 512, 2 * MEM_W, "mem_kv")
        mem_kv3 = mem_kv.reshape(B, MEM_LEN, 2 * MEM_W)
        if layer < N_A_LAYERS:
            h = _matmul(xf, a_w_in, layer, 512, a_cols // 2, "in_proj_a")
            h3 = h.reshape(B, S, a_cols)
            seq = _hgrn(h3, a_lower_bounds, a_gnorm[layer], layer)
            q_col_block = 4 * SEQ_MIX_W // MEM_W
        else:
            if layer == N_A_LAYERS:
                kv = _matmul(xf, w_kv_shared[None], 0, 512, SEQ_MIX_W, "kv_shared", BF16)
                kv3 = kv.reshape(B, S, 2 * SEQ_MIX_W)
            h = _matmul(xf, b_w_in, layer - N_A_LAYERS, 512, b_cols, "in_proj_b", BF16)
            h3 = h.reshape(B, S, b_cols)
            seq = _sb_attention(h3, kv3)
            q_col_block = SEQ_MIX_W // MEM_W
        memo = _mem_attention(h3, mem_kv3, q_col_block, 512)
        x1, gates, dest_rows, counts8 = _oproj_ln_route(
            seq.reshape(T, SEQ_MIX_W), memo.reshape(T, MEM_W), xf, w_o, layer,
            ln_mix_g[layer], ln_mix_b[layer],
            w_group[layer], b_group[layer], w_router[layer], b_router[layer])
        xf = _moe_ln(x1, gates, dest_rows, counts8, w_gate, w_up, w_down, layer,
                     ln_ffn_g[layer], ln_ffn_b[layer])
    return xf.reshape(B, S, D)
```

```python
import functools
import math

import numpy as np
import jax
import jax.numpy as jnp
from jax import lax
from jax.experimental import pallas as pl
from jax.experimental.pallas import tpu as pltpu

F32 = jnp.float32
BF16 = jnp.bfloat16

D_MODEL = 1024
DEPTH = 4
N_A_LAYERS = DEPTH // 2
MEM_LEN = 256
MEM_HEADS = 4
HEAD_DIM = 64
MEM_W = MEM_HEADS * HEAD_DIM
SEQ_MIX_W = D_MODEL - MEM_W
HGRN_DK = 128
HGRN_HEADS = SEQ_MIX_W // HGRN_DK
SB_HEADS = SEQ_MIX_W // HEAD_DIM
N_GROUPS = 4
EXPERTS_PER_GROUP = 4
N_EXPERTS = N_GROUPS * EXPERTS_PER_GROUP
D_EXPERT = 256
DEEPNORM_ALPHA = (2 * DEPTH) ** 0.25
LN_EPS = 1e-5
RMS_EPS = 1e-6

LANES = 128
HGRN_C = 128
SB_TK = 256
SB_ZERO_MASS = 110.0
MOE_TM = 512
MOE_R = 144
MOE_ALIGN = 16
MOE_MAX_PAIRS = MOE_TM // MOE_R + N_GROUPS
MOE_SORT_ROWS = -(-(MOE_TM + N_GROUPS * MOE_ALIGN + MOE_R) // 256) * 256
MOE_DEST_LANE = N_EXPERTS
MIB = 1024 * 1024


def _cparams(semantics, vmem_mib):
    return pltpu.CompilerParams(dimension_semantics=semantics, vmem_limit_bytes=vmem_mib * MIB)


def _dot(a, b):
    return jnp.dot(a, b, preferred_element_type=F32)


def _dot_nt(a, b):
    return lax.dot_general(a, b, (((1,), (1,)), ((), ())), preferred_element_type=F32)


def _dot_tn(a, b):
    return lax.dot_general(a, b, (((0,), (0,)), ((), ())), preferred_element_type=F32)


def _split_bf16(x):
    hi = x.astype(BF16)
    lo = (x - hi.astype(F32)).astype(BF16)
    return hi, lo


def _layer_norm(r, g, b):
    mu = jnp.mean(r, axis=-1, keepdims=True)
    d = r - mu
    var = jnp.mean(d * d, axis=-1, keepdims=True)
    return d * lax.rsqrt(var + LN_EPS) * g + b


def _mm_kernel(x_ref, w_ref, o_ref, wb_ref):
    @pl.when(pl.program_id(1) == 0)
    def _():
        wb_ref[...] = w_ref[...].astype(BF16)

    o_ref[...] = _dot(x_ref[...].astype(BF16), wb_ref[...]).astype(o_ref.dtype)


def _matmul(x, w_stack, layer, tm, tn, name, out_dtype=F32):
    M, K = x.shape
    N = w_stack.shape[2]
    assert M % tm == 0 and N % tn == 0
    return pl.pallas_call(
        _mm_kernel,
        grid=(N // tn, M // tm),
        in_specs=[pl.BlockSpec((tm, K), lambda j, i: (i, 0)),
                  pl.BlockSpec((None, K, tn), lambda j, i: (layer, 0, j))],
        out_specs=pl.BlockSpec((tm, tn), lambda j, i: (i, j)),
        out_shape=jax.ShapeDtypeStruct((M, N), out_dtype),
        scratch_shapes=[pltpu.VMEM((K, tn), BF16)],
        compiler_params=_cparams(("parallel", "arbitrary"), 48),
        name=name,
    )(x, w_stack)


def _hgrn_levels():
    m = HGRN_C // 2
    out = []
    while m >= 1:
        out.append(m)
        m //= 2
    return out


def _hgrn_level_decay(m, G, f, row):
    C = HGRN_C
    if m >= 4:
        G3 = G.reshape(C // (2 * m), 2 * m, LANES)
        return jnp.exp(-jnp.abs(G3 - G3[:, m - 1:m, :])).reshape(C, LANES)
    if m == 2:
        f_next = pltpu.roll(f, C - 1, axis=0)
        f_prev = pltpu.roll(f, 1, axis=0)
        p = row & 3
        return jnp.where(p == 0, f_next, jnp.where(p == 1, 1.0, jnp.where(p == 2, f, f * f_prev)))
    return jnp.where((row & 1) == 1, f, 1.0)


def _hgrn_kernel(layer, q_ref, f_ref, i_ref, g_ref, lbp_ref, gn_ref, incl_ref, o_ref, st_ref):
    C = HGRN_C

    @pl.when(pl.program_id(1) == 0)
    def _():
        st_ref[...] = jnp.zeros_like(st_ref)

    a = lbp_ref[...]
    ea = jnp.exp(a - jnp.max(a, axis=0, keepdims=True))
    pa = ea / jnp.sum(ea, axis=0, keepdims=True)
    lb_all = jnp.zeros((1, SEQ_MIX_W), F32)
    for l in range(1, layer + 1):
        lb_all = lb_all + pa[l:l + 1]

    t_idx = lax.broadcasted_iota(jnp.int32, (C, C), 0)
    s_idx = lax.broadcasted_iota(jnp.int32, (C, C), 1)
    split = t_idx ^ s_idx
    levels = _hgrn_levels()
    incl = incl_ref[...]
    row = lax.broadcasted_iota(jnp.int32, (C, LANES), 0)

    for h in range(HGRN_HEADS):
        lanes = slice(h * LANES, (h + 1) * LANES)
        lb = lb_all[:, lanes]
        fr = f_ref[:, lanes]
        logsig = jnp.minimum(fr, 0.0) - jnp.log(1.0 + jnp.exp(-jnp.abs(fr)))
        la = jnp.log(lb)
        lc = jnp.log1p(-lb) + logsig
        logf = jnp.maximum(la, lc) + jnp.log(1.0 + jnp.exp(-jnp.abs(la - lc)))
        f = jnp.exp(logf)
        k = 1.0 - f

        hi = logf.astype(BF16)
        r1 = logf - hi.astype(F32)
        mid = r1.astype(BF16)
        lo = (r1 - mid.astype(F32)).astype(BF16)
        g3 = _dot(incl, jnp.concatenate([hi, mid, lo], axis=1))
        G = g3[:, :LANES] + g3[:, LANES:2 * LANES] + g3[:, 2 * LANES:]
        dec_g = jnp.exp(G)
        dec_suf = jnp.exp(G[C - 1:C, :] - G)

        qr = q_ref[:, lanes]
        q = qr * jax.nn.sigmoid(qr) * (HGRN_DK ** -0.5)
        v = i_ref[:, lanes].astype(BF16)

        sc = None
        for li in range(len(levels) - 1, -1, -1):
            m = levels[li]
            u = (jnp.where((row & m) != 0, q, k) * _hgrn_level_decay(m, G, f, row)).astype(BF16)
            p = _dot_nt(u, u)
            sc = p if sc is None else jnp.where(split >= m, p, sc)
        sc = jnp.where(s_idx < t_idx, sc, 0.0)
        sc = jnp.where(s_idx == t_idx, jnp.sum(q * k, axis=-1, keepdims=True), sc)

        st = st_ref[h]
        o = _dot(sc.astype(BF16), v) + _dot_nt((q * dec_g).astype(BF16), st.astype(BF16))
        st_ref[h] = st * dec_g[C - 1:C, :] + _dot_tn(v, (k * dec_suf).astype(BF16))

        o = o * lax.rsqrt(jnp.mean(o * o, axis=-1, keepdims=True) + RMS_EPS)
        gr = g_ref[:, lanes]
        o_ref[:, lanes] = (o * gn_ref[:, lanes] * (gr * jax.nn.sigmoid(gr))).astype(o_ref.dtype)


def _hgrn(h3, a_lower_bounds, gnorm, layer):
    B, S, _ = h3.shape
    C = HGRN_C
    W = SEQ_MIX_W
    incl = jnp.asarray(np.arange(C)[None, :] <= np.arange(C)[:, None], dtype=BF16)

    def col(j):
        return pl.BlockSpec((None, C, W), lambda b, c: (b, c, j))

    return pl.pallas_call(
        functools.partial(_hgrn_kernel, layer),
        grid=(B, S // C),
        in_specs=[col(0), col(1), col(2), col(3),
                  pl.BlockSpec((N_A_LAYERS, W), lambda b, c: (0, 0)),
                  pl.BlockSpec((1, W), lambda b, c: (0, 0)),
                  pl.BlockSpec((C, C), lambda b, c: (0, 0))],
        out_specs=pl.BlockSpec((None, C, W), lambda b, c: (b, c, 0)),
        out_shape=jax.ShapeDtypeStruct((B, S, W), BF16),
        scratch_shapes=[pltpu.VMEM((HGRN_HEADS, LANES, HGRN_DK), F32)],
        compiler_params=_cparams(("parallel", "arbitrary"), 32),
        name="hgrn2",
    )(h3, h3, h3, h3, a_lower_bounds, gnorm.reshape(1, W), incl)


def _sb_kernel(q_ref, k_ref, v_ref, later_ref, o_ref, acc_ref, carry_ref):
    TK = SB_TK
    TQ = 2 * TK
    qi = pl.program_id(2)
    lane = lax.broadcasted_iota(jnp.int32, (TQ, LANES), 1)
    q = q_ref[...] * (1.0 / math.sqrt(HEAD_DIM))
    zero = jnp.zeros_like(q)
    qh = [jnp.where(lane < HEAD_DIM, q, zero), jnp.where(lane >= HEAD_DIM, q, zero)]
    later = later_ref[...]
    row = lax.broadcasted_iota(jnp.int32, (TK, TK), 0)
    col = lax.broadcasted_iota(jnp.int32, (TK, TK), 1)
    causal = col < row

    def sweep(h, rows, kb, masked, carry):
        start = pl.multiple_of(kb * TK, TK)
        z = _dot_nt(qh[h][rows], k_ref[pl.ds(start, TK), :])
        sp = jnp.maximum(z, 0.0) + jnp.log(1.0 + jnp.exp(-jnp.abs(z)))
        spm = jnp.where(causal, sp, 0.0) if masked else sp
        rest = _dot(spm.astype(BF16), later) + carry
        a = jnp.exp(z - sp - rest)
        if masked:
            a = jnp.where(causal, a, 0.0)
        contrib = _dot(a.astype(BF16), v_ref[pl.ds(start, TK), :])
        return contrib, carry + jnp.sum(spm, axis=1, keepdims=True)

    top = slice(0, TK)
    bot = slice(TK, TQ)
    no_mass = jnp.zeros((TK, 1), F32)
    has_prev = qi > 0
    for h in range(2):
        acc_t, c_t = sweep(h, top, 2 * qi, True, no_mass)
        acc_t2, c_t2 = sweep(h, top, jnp.maximum(2 * qi - 1, 0), False, c_t)
        acc_b, c_b = sweep(h, bot, 2 * qi + 1, True, no_mass)
        acc_b2, c_b = sweep(h, bot, 2 * qi, False, c_b)
        acc_ref[h, top, :] = acc_t + jnp.where(has_prev, acc_t2, 0.0)
        acc_ref[h, bot, :] = acc_b + acc_b2
        carry_ref[h, top, :] = jnp.where(has_prev, c_t2, c_t)
        carry_ref[h, bot, :] = c_b

    def min_mass():
        return jnp.minimum(jnp.min(carry_ref[0]), jnp.min(carry_ref[1]))

    def more(state):
        j, mass = state
        return jnp.logical_and(j < 2 * qi, mass < SB_ZERO_MASS)

    def earlier(state):
        j, _ = state
        kb_bot = 2 * qi - 1 - j
        kb_top = kb_bot - 1
        top_live = kb_top >= 0
        for h in range(2):
            contrib, c = sweep(h, top, jnp.maximum(kb_top, 0), False, carry_ref[h, top, :])
            acc_ref[h, top, :] += jnp.where(top_live, contrib, 0.0)
            carry_ref[h, top, :] = jnp.where(top_live, c, carry_ref[h, top, :])
            contrib, c = sweep(h, bot, kb_bot, False, carry_ref[h, bot, :])
            acc_ref[h, bot, :] += contrib
            carry_ref[h, bot, :] = c
        return j + 1, min_mass()

    lax.while_loop(more, earlier, (jnp.int32(0), min_mass()))
    o_ref[...] = jnp.where(lane < HEAD_DIM, acc_ref[0], acc_ref[1]).astype(o_ref.dtype)


def _sb_attention(h3, kv3):
    B, S, _ = h3.shape
    TK = SB_TK
    TQ = 2 * TK
    pairs = SEQ_MIX_W // LANES
    later = jnp.asarray(np.arange(TK)[:, None] > np.arange(TK)[None, :], dtype=BF16)
    return pl.pallas_call(
        _sb_kernel,
        grid=(B, pairs, S // TQ),
        in_specs=[pl.BlockSpec((None, TQ, LANES), lambda b, p, i: (b, i, p)),
                  pl.BlockSpec((None, S, LANES), lambda b, p, i: (b, 0, p)),
                  pl.BlockSpec((None, S, LANES), lambda b, p, i: (b, 0, pairs + p)),
                  pl.BlockSpec((TK, TK), lambda b, p, i: (0, 0))],
        out_specs=pl.BlockSpec((None, TQ, LANES), lambda b, p, i: (b, i, p)),
        out_shape=jax.ShapeDtypeStruct((B, S, SEQ_MIX_W), BF16),
        scratch_shapes=[pltpu.VMEM((2, TQ, LANES), F32), pltpu.VMEM((2, TQ, 1), F32)],
        compiler_params=_cparams(("parallel", "parallel", "arbitrary"), 32),
        name="stick_breaking",
    )(h3, kv3, kv3, later)


def _mem_kernel(q_ref, kv_ref, o_ref):
    q = q_ref[...] * (1.0 / math.sqrt(HEAD_DIM))
    outs = []
    for h in range(MEM_HEADS):
        lanes = slice(h * HEAD_DIM, (h + 1) * HEAD_DIM)
        vl = slice(MEM_W + h * HEAD_DIM, MEM_W + (h + 1) * HEAD_DIM)
        s = _dot_nt(q[:, lanes].astype(BF16), kv_ref[:, lanes].astype(BF16))
        e = jnp.exp(s - jnp.max(s, axis=-1, keepdims=True))
        den = jnp.sum(e, axis=-1, keepdims=True)
        outs.append(_dot(e.astype(BF16), kv_ref[:, vl].astype(BF16)) / den)
    o_ref[...] = jnp.concatenate(outs, axis=1).astype(o_ref.dtype)


def _mem_attention(h3, mem_kv3, q_col_block, tm):
    B, S, _ = h3.shape
    return pl.pallas_call(
        _mem_kernel,
        grid=(B, S // tm),
        in_specs=[pl.BlockSpec((None, tm, MEM_W), lambda b, i: (b, i, q_col_block)),
                  pl.BlockSpec((None, MEM_LEN, 2 * MEM_W), lambda b, i: (b, 0, 0))],
        out_specs=pl.BlockSpec((None, tm, MEM_W), lambda b, i: (b, i, 0)),
        out_shape=jax.ShapeDtypeStruct((B, S, MEM_W), BF16),
        compiler_params=_cparams(("parallel", "parallel"), 32),
        name="mem_attention",
    )(h3, mem_kv3)


def _route(lg):
    lane = lax.broadcasted_iota(jnp.int32, lg.shape, 1)
    lane_f = lane.astype(F32)
    neg = -jnp.inf
    far = float(LANES)

    lgg = jnp.where((lane >= N_EXPERTS) & (lane < N_EXPERTS + N_GROUPS), lg, neg)
    gmax = jnp.max(lgg, axis=-1, keepdims=True)
    p_top = 1.0 / jnp.sum(jnp.exp(lgg - gmax), axis=-1, keepdims=True)
    g_lane = jnp.min(jnp.where(lgg == gmax, lane_f, far), axis=-1, keepdims=True)
    first = (g_lane - float(N_EXPERTS)) * float(EXPERTS_PER_GROUP)

    le = jnp.where((lane_f >= first) & (lane_f < first + float(EXPERTS_PER_GROUP)), lg, neg)
    m1 = jnp.max(le, axis=-1, keepdims=True)
    i1 = jnp.min(jnp.where(le == m1, lane_f, far), axis=-1, keepdims=True)
    le2 = jnp.where(lane_f == i1, neg, le)
    m2 = jnp.max(le2, axis=-1, keepdims=True)
    i2 = jnp.min(jnp.where(le2 == m2, lane_f, far), axis=-1, keepdims=True)
    e2 = jnp.exp(m2 - m1)
    den = 1.0 + e2
    gates = p_top * (jnp.where(lane_f == i1, 1.0 / den, 0.0) + jnp.where(lane_f == i2, e2 / den, 0.0))
    return gates, g_lane - float(N_EXPERTS)


def _oproj_kernel(seq_ref, mem_ref, x_ref, wo_ref, g_ref, b_ref, wr_ref, br_ref, before_ref,
                  o_ref, gates_ref, drow_ref, cnt_ref, wob_ref, wrh_ref, wrl_ref):
    @pl.when(pl.program_id(0) == 0)
    def _():
        wob_ref[...] = wo_ref[...].astype(BF16)
        hi, lo = _split_bf16(wr_ref[...])
        wrh_ref[...] = hi
        wrl_ref[...] = lo

    y = (_dot(seq_ref[...], wob_ref[0:SEQ_MIX_W, :]) + _dot(mem_ref[...], wob_ref[SEQ_MIX_W:, :]))
    x1 = _layer_norm(DEEPNORM_ALPHA * x_ref[...] + y, g_ref[...], b_ref[...])
    o_ref[...] = x1
    xh, xl = _split_bf16(x1)
    wrh = wrh_ref[...]
    lg2 = _dot(xh, jnp.concatenate([wrh, wrl_ref[...]], axis=1))
    gates, grp = _route(lg2[:, :LANES] + lg2[:, LANES:] + _dot(xl, wrh) + br_ref[...])

    lane_f = lax.broadcasted_iota(jnp.int32, gates.shape, 1).astype(F32)
    onehot = jnp.where(lane_f == grp, 1.0, 0.0)
    ranks = _dot(before_ref[...], onehot.astype(BF16))
    rank = jnp.sum(jnp.where(lane_f == grp, ranks, 0.0), axis=-1, keepdims=True)
    counts = jnp.sum(onehot, axis=0, keepdims=True)
    padded = jnp.floor((counts + float(MOE_ALIGN - 1)) * (1.0 / MOE_ALIGN)) * float(MOE_ALIGN)
    offset = jnp.sum(jnp.where(lane_f < grp, padded, 0.0), axis=-1, keepdims=True)
    dest = offset + rank
    gates_ref[...] = gates + jnp.where(lane_f == float(MOE_DEST_LANE), dest, 0.0)
    cnt_ref[...] = jnp.broadcast_to(counts, cnt_ref.shape)
    hi16 = jnp.floor(dest * (1.0 / 16.0))
    digits = (jnp.where(lane_f == 0.0, hi16, 0.0) + jnp.where(lane_f == 1.0, dest - 16.0 * hi16, 0.0))
    sel_lane = lax.broadcasted_iota(jnp.int32, (8, LANES), 1)
    sel = jnp.where(sel_lane == 0, 16.0, jnp.where(sel_lane == 1, 1.0, 0.0)).astype(BF16)
    drow_ref[...] = _dot_nt(sel, digits.astype(BF16))


def _oproj_ln_route(seq, memo, x, w_o, layer, g, b, w_group, b_group, w_router, b_router):
    T = x.shape[0]
    tm = MOE_TM
    pad = LANES - N_EXPERTS - N_GROUPS
    wr = jnp.concatenate([w_router, w_group, jnp.zeros((D_MODEL, pad), F32)], axis=1)
    br = jnp.concatenate([b_router, b_group, jnp.zeros((pad,), F32)]).reshape(1, LANES)
    before = jnp.asarray(np.arange(tm)[None, :] < np.arange(tm)[:, None], dtype=BF16)
    const = lambda i: (0, 0)
    return pl.pallas_call(
        _oproj_kernel,
        grid=(T // tm,),
        in_specs=[pl.BlockSpec((tm, SEQ_MIX_W), lambda i: (i, 0)),
                  pl.BlockSpec((tm, MEM_W), lambda i: (i, 0)),
                  pl.BlockSpec((tm, D_MODEL), lambda i: (i, 0)),
                  pl.BlockSpec((None, D_MODEL, D_MODEL), lambda i: (layer, 0, 0)),
                  pl.BlockSpec((1, D_MODEL), const),
                  pl.BlockSpec((1, D_MODEL), const),
                  pl.BlockSpec((D_MODEL, LANES), const),
                  pl.BlockSpec((1, LANES), const),
                  pl.BlockSpec((tm, tm), const)],
        out_specs=[pl.BlockSpec((tm, D_MODEL), lambda i: (i, 0)),
                   pl.BlockSpec((tm, LANES), lambda i: (i, 0)),
                   pl.BlockSpec((8, tm), lambda i: (i, 0)),
                   pl.BlockSpec((8, LANES), lambda i: (i, 0))],
        out_shape=[jax.ShapeDtypeStruct((T, D_MODEL), F32),
                   jax.ShapeDtypeStruct((T, LANES), F32),
                   jax.ShapeDtypeStruct((8 * (T // tm), tm), F32),
                   jax.ShapeDtypeStruct((8 * (T // tm), LANES), F32)],
        scratch_shapes=[pltpu.VMEM((D_MODEL, D_MODEL), BF16),
                        pltpu.VMEM((D_MODEL, LANES), BF16),
                        pltpu.VMEM((D_MODEL, LANES), BF16)],
        compiler_params=_cparams(("arbitrary",), 40),
        name="oproj_ln_route",
    )(seq, memo, x, w_o, g.reshape(1, D_MODEL), b.reshape(1, D_MODEL), wr, br, before)


def _moe_kernel(npairs_ref, pstart_ref, pgrp_ref,
                x_ref, gates_ref, drow_ref, wg_ref, wu_ref, wd_ref, g_ref, b_ref, o_ref,
                wgu_s, wd_s, xs_ref, gs_ref, ys_ref):
    step = pl.program_id(0)
    tm = MOE_TM
    R = MOE_R
    nrows = MOE_SORT_ROWS

    @pl.when(step < N_EXPERTS)
    def _():
        wgu_s[step, :, 0:D_EXPERT] = wg_ref[...].astype(BF16)
        wgu_s[step, :, D_EXPERT:] = wu_ref[...].astype(BF16)
        row0 = pl.multiple_of((step % EXPERTS_PER_GROUP) * D_EXPERT, D_EXPERT)
        wd_s[step // EXPERTS_PER_GROUP, pl.ds(row0, D_EXPERT), :] = wd_ref[...].astype(BF16)

    @pl.when(step >= N_EXPERTS)
    def _():
        tile = step - N_EXPERTS
        x = x_ref[...]
        gates = gates_ref[...]
        lane_g = lax.broadcasted_iota(jnp.int32, gates.shape, 1)
        dest_col = jnp.sum(jnp.where(lane_g == MOE_DEST_LANE, gates, 0.0), axis=-1, keepdims=True)
        sorted_row = lax.broadcasted_iota(jnp.int32, (nrows, tm), 0).astype(F32)
        perm = jnp.where(sorted_row == drow_ref[0:1, :], 1.0, 0.0).astype(BF16)
        xs_ref[...] = _dot(perm, x.astype(BF16)).astype(BF16)
        ghi, glo = _split_bf16(gates)
        g2 = _dot(perm, jnp.concatenate([ghi, glo], axis=1))
        gs_ref[...] = g2[:, :LANES] + g2[:, LANES:]
        ys_ref[...] = jnp.zeros_like(ys_ref)

        def pair(p, _):
            start = pstart_ref[tile * MOE_MAX_PAIRS + p]
            grp = pgrp_ref[tile * MOE_MAX_PAIRS + p]
            rows = pl.ds(pl.multiple_of(start, MOE_ALIGN), R)
            xblk = xs_ref[rows, :]
            gblk = gs_ref[rows, :]
            lane = lax.broadcasted_iota(jnp.int32, gblk.shape, 1)
            hs = []
            for j in range(EXPERTS_PER_GROUP):
                e = grp * EXPERTS_PER_GROUP + j
                hgu = _dot(xblk, wgu_s[e])
                hg = hgu[:, :D_EXPERT]
                ge = jnp.sum(jnp.where(lane == e, gblk, 0.0), axis=-1, keepdims=True)
                hs.append((hg * jax.nn.sigmoid(hg) * hgu[:, D_EXPERT:] * ge).astype(BF16))
            ys_ref[rows, :] += _dot(jnp.concatenate(hs, axis=1), wd_s[grp])
            return 0

        lax.fori_loop(0, npairs_ref[tile], pair, 0)

        sorted_col = lax.broadcasted_iota(jnp.int32, (tm, nrows), 1).astype(F32)
        unperm = jnp.where(sorted_col == dest_col, 1.0, 0.0).astype(BF16)
        y = _dot(unperm, ys_ref[...].astype(BF16))
        o_ref[...] = _layer_norm(DEEPNORM_ALPHA * x + y, g_ref[...], b_ref[...])


def _moe_plan(counts):
    ntiles = counts.shape[0]
    max_blocks = -(-MOE_TM // MOE_R)
    padded = (counts + (MOE_ALIGN - 1)) // MOE_ALIGN * MOE_ALIGN
    seg_start = jnp.cumsum(padded, axis=1) - padded
    nblk = (counts + (MOE_R - 1)) // MOE_R
    j = jnp.arange(max_blocks, dtype=jnp.int32)[None, None, :]
    active = (j < nblk[:, :, None]).reshape(ntiles, N_GROUPS * max_blocks)
    order = jnp.argsort(jnp.logical_not(active), axis=1, stable=True)[:, :MOE_MAX_PAIRS].astype(jnp.int32)
    grp = order // max_blocks
    start = jnp.take_along_axis(seg_start, grp, axis=1) + (order % max_blocks) * MOE_R
    npairs = jnp.sum(active, axis=1).astype(jnp.int32)
    return npairs, start.astype(jnp.int32).reshape(-1), grp.reshape(-1)


def _moe_ln(x, gates, dest_rows, counts8, w_gate, w_up, w_down, layer, g, b):
    T = x.shape[0]
    tm = MOE_TM
    ntiles = T // tm
    counts = counts8.reshape(ntiles, 8, LANES)[:, 0, :N_GROUPS].astype(jnp.int32)
    npairs, pstart, pgrp = _moe_plan(counts)

    def tile_map(s, *_):
        return (jnp.maximum(s - N_EXPERTS, 0), 0)

    def w_map(s, *_):
        return (layer, jnp.minimum(s, N_EXPERTS - 1), 0, 0)

    const = lambda s, *_: (0, 0)
    grid_spec = pltpu.PrefetchScalarGridSpec(
        num_scalar_prefetch=3,
        grid=(N_EXPERTS + ntiles,),
        in_specs=[pl.BlockSpec((tm, D_MODEL), tile_map),
                  pl.BlockSpec((tm, LANES), tile_map),
                  pl.BlockSpec((8, tm), tile_map),
                  pl.BlockSpec((None, None, D_MODEL, D_EXPERT), w_map),
                  pl.BlockSpec((None, None, D_MODEL, D_EXPERT), w_map),
                  pl.BlockSpec((None, None, D_EXPERT, D_MODEL), w_map),
                  pl.BlockSpec((1, D_MODEL), const),
                  pl.BlockSpec((1, D_MODEL), const)],
        out_specs=pl.BlockSpec((tm, D_MODEL), tile_map),
        scratch_shapes=[pltpu.VMEM((N_EXPERTS, D_MODEL, 2 * D_EXPERT), BF16),
                        pltpu.VMEM((N_GROUPS, EXPERTS_PER_GROUP * D_EXPERT, D_MODEL), BF16),
                        pltpu.VMEM((MOE_SORT_ROWS, D_MODEL), BF16),
                        pltpu.VMEM((MOE_SORT_ROWS, LANES), F32),
                        pltpu.VMEM((MOE_SORT_ROWS, D_MODEL), F32)])
    return pl.pallas_call(
        _moe_kernel,
        grid_spec=grid_spec,
        out_shape=jax.ShapeDtypeStruct((T, D_MODEL), F32),
        compiler_params=_cparams(("arbitrary",), 56),
        name="moe_ln",
    )(npairs, pstart, pgrp, x, gates, dest_rows, w_gate, w_up, w_down,
      g.reshape(1, D_MODEL), b.reshape(1, D_MODEL))


def kernel(x, mem, a_w_in, a_lower_bounds, a_gnorm, b_w_in, w_kv_shared, w_mem_kv, w_o,
           ln_mix_g, ln_mix_b, ln_ffn_g, ln_ffn_b, w_group, b_group, w_router, b_router,
           w_gate, w_up, w_down):
    B, S, D = x.shape
    T = B * S
    xf = x.reshape(T, D)
    memf = mem.reshape(B * MEM_LEN, D)
    a_cols = a_w_in.shape[-1]
    b_cols = b_w_in.shape[-1]
    kv3 = None
    for layer in range(DEPTH):
        mem_kv = _matmul(memf, w_mem_kv, layer, 512, 2 * MEM_W, "mem_kv")
        mem_kv3 = mem_kv.reshape(B, MEM_LEN, 2 * MEM_W)
        if layer < N_A_LAYERS:
            h = _matmul(xf, a_w_in, layer, 512, a_cols // 2, "in_proj_a")
            h3 = h.reshape(B, S, a_cols)
            seq = _hgrn(h3, a_lower_bounds, a_gnorm[layer], layer)
            q_col_block = 4 * SEQ_MIX_W // MEM_W
        else:
            if layer == N_A_LAYERS:
                kv = _matmul(xf, w_kv_shared[None], 0, 512, SEQ_MIX_W, "kv_shared", BF16)
                kv3 = kv.reshape(B, S, 2 * SEQ_MIX_W)
            h = _matmul(xf, b_w_in, layer - N_A_LAYERS, 512, b_cols, "in_proj_b", BF16)
            h3 = h.reshape(B, S, b_cols)
            seq = _sb_attention(h3, kv3)
            q_col_block = SEQ_MIX_W // MEM_W
        memo = _mem_attention(h3, mem_kv3, q_col_block, 512)
        x1, gates, dest_rows, counts8 = _oproj_ln_route(
            seq.reshape(T, SEQ_MIX_W), memo.reshape(T, MEM_W), xf, w_o, layer,
            ln_mix_g[layer], ln_mix_b[layer],
            w_group[layer], b_group[layer], w_router[layer], b_router[layer])
        xf = _moe_ln(x1, gates, dest_rows, counts8, w_gate, w_up, w_down, layer,
                     ln_ffn_g[layer], ln_ffn_b[layer])
    return xf.reshape(B, S, D)
```

```python
import functools
import math

import numpy as np
import jax
import jax.numpy as jnp
from jax import lax
from jax.experimental import pallas as pl
from jax.experimental.pallas import tpu as pltpu

F32 = jnp.float32
BF16 = jnp.bfloat16

D_MODEL = 1024
DEPTH = 4
N_A_LAYERS = DEPTH // 2
MEM_LEN = 256
MEM_HEADS = 4
HEAD_DIM = 64
MEM_W = MEM_HEADS * HEAD_DIM
SEQ_MIX_W = D_MODEL - MEM_W
HGRN_DK = 128
HGRN_HEADS = SEQ_MIX_W // HGRN_DK
SB_HEADS = SEQ_MIX_W // HEAD_DIM
N_GROUPS = 4
EXPERTS_PER_GROUP = 4
N_EXPERTS = N_GROUPS * EXPERTS_PER_GROUP
D_EXPERT = 256
DEEPNORM_ALPHA = (2 * DEPTH) ** 0.25
LN_EPS = 1e-5
RMS_EPS = 1e-6

LANES = 128
HGRN_C = 128
SB_TK = 256
SB_ZERO_MASS = 110.0
MOE_TM = 512
MOE_R = 144
MOE_ALIGN = 16
MOE_MAX_PAIRS = MOE_TM // MOE_R + N_GROUPS
MOE_SORT_ROWS = -(-(MOE_TM + N_GROUPS * MOE_ALIGN + MOE_R) // 256) * 256
MOE_DEST_LANE = N_EXPERTS
MIB = 1024 * 1024


def _cparams(semantics, vmem_mib):
    return pltpu.CompilerParams(dimension_semantics=semantics, vmem_limit_bytes=vmem_mib * MIB)


def _dot(a, b):
    return jnp.dot(a, b, preferred_element_type=F32)


def _dot_nt(a, b):
    return lax.dot_general(a, b, (((1,), (1,)), ((), ())), preferred_element_type=F32)


def _dot_tn(a, b):
    return lax.dot_general(a, b, (((0,), (0,)), ((), ())), preferred_element_type=F32)


def _split_bf16(x):
    hi = x.astype(BF16)
    lo = (x - hi.astype(F32)).astype(BF16)
    return hi, lo


def _layer_norm(r, g, b):
    mu = jnp.mean(r, axis=-1, keepdims=True)
    d = r - mu
    var = jnp.mean(d * d, axis=-1, keepdims=True)
    return d * lax.rsqrt(var + LN_EPS) * g + b


def _mm_kernel(x_ref, w_ref, o_ref, wb_ref):
    @pl.when(pl.program_id(1) == 0)
    def _():
        wb_ref[...] = w_ref[...].astype(BF16)

    o_ref[...] = _dot(x_ref[...].astype(BF16), wb_ref[...]).astype(o_ref.dtype)


def _matmul(x, w_stack, layer, tm, tn, name, out_dtype=F32):
    M, K = x.shape
    N = w_stack.shape[2]
    assert M % tm == 0 and N % tn == 0
    return pl.pallas_call(
        _mm_kernel,
        grid=(N // tn, M // tm),
        in_specs=[pl.BlockSpec((tm, K), lambda j, i: (i, 0)),
                  pl.BlockSpec((None, K, tn), lambda j, i: (layer, 0, j))],
        out_specs=pl.BlockSpec((tm, tn), lambda j, i: (i, j)),
        out_shape=jax.ShapeDtypeStruct((M, N), out_dtype),
        scratch_shapes=[pltpu.VMEM((K, tn), BF16)],
        compiler_params=_cparams(("parallel", "arbitrary"), 48),
        name=name,
    )(x, w_stack)


def _in_proj_a_kernel(x_ref, w_ref, o_ref, f_ref, wb_ref):
    @pl.when(pl.program_id(0) == 0)
    def _():
        wb_ref[...] = w_ref[...].astype(BF16)

    acc = _dot(x_ref[...].astype(BF16), wb_ref[...])
    o_ref[...] = acc.astype(BF16)
    f_ref[...] = acc[:, SEQ_MIX_W:2 * SEQ_MIX_W]


def _in_proj_a(x, w_stack, layer, tm):
    M, K = x.shape
    N = w_stack.shape[2]
    return pl.pallas_call(
        _in_proj_a_kernel,
        grid=(M // tm,),
        in_specs=[pl.BlockSpec((tm, K), lambda i: (i, 0)),
                  pl.BlockSpec((None, K, N), lambda i: (layer, 0, 0))],
        out_specs=[pl.BlockSpec((tm, N), lambda i: (i, 0)),
                   pl.BlockSpec((tm, SEQ_MIX_W), lambda i: (i, 0))],
        out_shape=[jax.ShapeDtypeStruct((M, N), BF16),
                   jax.ShapeDtypeStruct((M, SEQ_MIX_W), F32)],
        scratch_shapes=[pltpu.VMEM((K, N), BF16)],
        compiler_params=_cparams(("arbitrary",), 52),
        name="in_proj_a",
    )(x, w_stack)


def _hgrn_levels():
    m = HGRN_C // 2
    out = []
    while m >= 1:
        out.append(m)
        m //= 2
    return out


def _hgrn_level_decay(m, G, f, row):
    C = HGRN_C
    if m >= 4:
        G3 = G.reshape(C // (2 * m), 2 * m, LANES)
        return jnp.exp(-jnp.abs(G3 - G3[:, m - 1:m, :])).reshape(C, LANES)
    if m == 2:
        f_next = pltpu.roll(f, C - 1, axis=0)
        f_prev = pltpu.roll(f, 1, axis=0)
        p = row & 3
        return jnp.where(p == 0, f_next, jnp.where(p == 1, 1.0, jnp.where(p == 2, f, f * f_prev)))
    return jnp.where((row & 1) == 1, f, 1.0)


def _hgrn_kernel(layer, q_ref, f_ref, i_ref, g_ref, lbp_ref, gn_ref, incl_ref, o_ref, st_ref):
    C = HGRN_C

    @pl.when(pl.program_id(1) == 0)
    def _():
        st_ref[...] = jnp.zeros_like(st_ref)

    a = lbp_ref[...]
    ea = jnp.exp(a - jnp.max(a, axis=0, keepdims=True))
    pa = ea / jnp.sum(ea, axis=0, keepdims=True)
    lb_all = jnp.zeros((1, SEQ_MIX_W), F32)
    for l in range(1, layer + 1):
        lb_all = lb_all + pa[l:l + 1]

    t_idx = lax.broadcasted_iota(jnp.int32, (C, C), 0)
    s_idx = lax.broadcasted_iota(jnp.int32, (C, C), 1)
    split = t_idx ^ s_idx
    levels = _hgrn_levels()
    incl = incl_ref[...]
    row = lax.broadcasted_iota(jnp.int32, (C, LANES), 0)

    for h in range(HGRN_HEADS):
        lanes = slice(h * LANES, (h + 1) * LANES)
        lb = lb_all[:, lanes]
        fr = f_ref[:, lanes]
        logsig = jnp.minimum(fr, 0.0) - jnp.log(1.0 + jnp.exp(-jnp.abs(fr)))
        la = jnp.log(lb)
        lc = jnp.log1p(-lb) + logsig
        logf = jnp.maximum(la, lc) + jnp.log(1.0 + jnp.exp(-jnp.abs(la - lc)))
        f = jnp.exp(logf)
        k = 1.0 - f

        hi = logf.astype(BF16)
        r1 = logf - hi.astype(F32)
        mid = r1.astype(BF16)
        lo = (r1 - mid.astype(F32)).astype(BF16)
        g3 = _dot(incl, jnp.concatenate([hi, mid, lo], axis=1))
        G = g3[:, :LANES] + g3[:, LANES:2 * LANES] + g3[:, 2 * LANES:]
        dec_g = jnp.exp(G)
        dec_suf = jnp.exp(G[C - 1:C, :] - G)

        qr = q_ref[:, lanes].astype(F32)
        q = qr * jax.nn.sigmoid(qr) * (HGRN_DK ** -0.5)
        v = i_ref[:, lanes].astype(BF16)

        sc = None
        for li in range(len(levels) - 1, -1, -1):
            m = levels[li]
            u = (jnp.where((row & m) != 0, q, k) * _hgrn_level_decay(m, G, f, row)).astype(BF16)
            p = _dot_nt(u, u)
            sc = p if sc is None else jnp.where(split >= m, p, sc)
        sc = jnp.where(s_idx < t_idx, sc, 0.0)
        sc = jnp.where(s_idx == t_idx, jnp.sum(q * k, axis=-1, keepdims=True), sc)

        st = st_ref[h]
        o = _dot(sc.astype(BF16), v) + _dot_nt((q * dec_g).astype(BF16), st.astype(BF16))
        st_ref[h] = st * dec_g[C - 1:C, :] + _dot_tn(v, (k * dec_suf).astype(BF16))

        o = o * lax.rsqrt(jnp.mean(o * o, axis=-1, keepdims=True) + RMS_EPS)
        gr = g_ref[:, lanes].astype(F32)
        o_ref[:, lanes] = (o * gn_ref[:, lanes] * (gr * jax.nn.sigmoid(gr))).astype(o_ref.dtype)


def _hgrn(h3, f3, a_lower_bounds, gnorm, layer):
    B, S, _ = h3.shape
    C = HGRN_C
    W = SEQ_MIX_W
    incl = jnp.asarray(np.arange(C)[None, :] <= np.arange(C)[:, None], dtype=BF16)

    def col(j):
        return pl.BlockSpec((None, C, W), lambda b, c: (b, c, j))

    return pl.pallas_call(
        functools.partial(_hgrn_kernel, layer),
        grid=(B, S // C),
        in_specs=[col(0), col(0), col(2), col(3),
                  pl.BlockSpec((N_A_LAYERS, W), lambda b, c: (0, 0)),
                  pl.BlockSpec((1, W), lambda b, c: (0, 0)),
                  pl.BlockSpec((C, C), lambda b, c: (0, 0))],
        out_specs=pl.BlockSpec((None, C, W), lambda b, c: (b, c, 0)),
        out_shape=jax.ShapeDtypeStruct((B, S, W), BF16),
        scratch_shapes=[pltpu.VMEM((HGRN_HEADS, LANES, HGRN_DK), F32)],
        compiler_params=_cparams(("parallel", "arbitrary"), 32),
        name="hgrn2",
    )(h3, f3, h3, h3, a_lower_bounds, gnorm.reshape(1, W), incl)


def _sb_kernel(q_ref, k_ref, v_ref, later_ref, o_ref, acc_ref, carry_ref):
    TK = SB_TK
    TQ = 2 * TK
    qi = pl.program_id(2)
    lane = lax.broadcasted_iota(jnp.int32, (TQ, LANES), 1)
    q = q_ref[...] * (1.0 / math.sqrt(HEAD_DIM))
    zero = jnp.zeros_like(q)
    qh = [jnp.where(lane < HEAD_DIM, q, zero), jnp.where(lane >= HEAD_DIM, q, zero)]
    later = later_ref[...]
    row = lax.broadcasted_iota(jnp.int32, (TK, TK), 0)
    col = lax.broadcasted_iota(jnp.int32, (TK, TK), 1)
    causal = col < row

    def sweep(h, rows, kb, masked, carry):
        start = pl.multiple_of(kb * TK, TK)
        z = _dot_nt(qh[h][rows], k_ref[pl.ds(start, TK), :])
        sp = jnp.maximum(z, 0.0) + jnp.log(1.0 + jnp.exp(-jnp.abs(z)))
        spm = jnp.where(causal, sp, 0.0) if masked else sp
        rest = _dot(spm.astype(BF16), later) + carry
        a = jnp.exp(z - sp - rest)
        if masked:
            a = jnp.where(causal, a, 0.0)
        contrib = _dot(a.astype(BF16), v_ref[pl.ds(start, TK), :])
        return contrib, carry + jnp.sum(spm, axis=1, keepdims=True)

    top = slice(0, TK)
    bot = slice(TK, TQ)
    no_mass = jnp.zeros((TK, 1), F32)
    has_prev = qi > 0
    for h in range(2):
        acc_t, c_t = sweep(h, top, 2 * qi, True, no_mass)
        acc_t2, c_t2 = sweep(h, top, jnp.maximum(2 * qi - 1, 0), False, c_t)
        acc_b, c_b = sweep(h, bot, 2 * qi + 1, True, no_mass)
        acc_b2, c_b = sweep(h, bot, 2 * qi, False, c_b)
        acc_ref[h, top, :] = acc_t + jnp.where(has_prev, acc_t2, 0.0)
        acc_ref[h, bot, :] = acc_b + acc_b2
        carry_ref[h, top, :] = jnp.where(has_prev, c_t2, c_t)
        carry_ref[h, bot, :] = c_b

    def min_mass():
        return jnp.minimum(jnp.min(carry_ref[0]), jnp.min(carry_ref[1]))

    def more(state):
        j, mass = state
        return jnp.logical_and(j < 2 * qi, mass < SB_ZERO_MASS)

    def earlier(state):
        j, _ = state
        kb_bot = 2 * qi - 1 - j
        kb_top = kb_bot - 1
        top_live = kb_top >= 0
        for h in range(2):
            contrib, c = sweep(h, top, jnp.maximum(kb_top, 0), False, carry_ref[h, top, :])
            acc_ref[h, top, :] += jnp.where(top_live, contrib, 0.0)
            carry_ref[h, top, :] = jnp.where(top_live, c, carry_ref[h, top, :])
            contrib, c = sweep(h, bot, kb_bot, False, carry_ref[h, bot, :])
            acc_ref[h, bot, :] += contrib
            carry_ref[h, bot, :] = c
        return j + 1, min_mass()

    lax.while_loop(more, earlier, (jnp.int32(0), min_mass()))
    o_ref[...] = jnp.where(lane < HEAD_DIM, acc_ref[0], acc_ref[1]).astype(o_ref.dtype)


def _sb_attention(h3, kv3):
    B, S, _ = h3.shape
    TK = SB_TK
    TQ = 2 * TK
    pairs = SEQ_MIX_W // LANES
    later = jnp.asarray(np.arange(TK)[:, None] > np.arange(TK)[None, :], dtype=BF16)
    return pl.pallas_call(
        _sb_kernel,
        grid=(B, pairs, S // TQ),
        in_specs=[pl.BlockSpec((None, TQ, LANES), lambda b, p, i: (b, i, p)),
                  pl.BlockSpec((None, S, LANES), lambda b, p, i: (b, 0, p)),
                  pl.BlockSpec((None, S, LANES), lambda b, p, i: (b, 0, pairs + p)),
                  pl.BlockSpec((TK, TK), lambda b, p, i: (0, 0))],
        out_specs=pl.BlockSpec((None, TQ, LANES), lambda b, p, i: (b, i, p)),
        out_shape=jax.ShapeDtypeStruct((B, S, SEQ_MIX_W), BF16),
        scratch_shapes=[pltpu.VMEM((2, TQ, LANES), F32), pltpu.VMEM((2, TQ, 1), F32)],
        compiler_params=_cparams(("parallel", "parallel", "arbitrary"), 32),
        name="stick_breaking",
    )(h3, kv3, kv3, later)


def _mem_kernel(q_ref, kv_ref, o_ref):
    q = q_ref[...] * (1.0 / math.sqrt(HEAD_DIM))
    outs = []
    for h in range(MEM_HEADS):
        lanes = slice(h * HEAD_DIM, (h + 1) * HEAD_DIM)
        vl = slice(MEM_W + h * HEAD_DIM, MEM_W + (h + 1) * HEAD_DIM)
        s = _dot_nt(q[:, lanes].astype(BF16), kv_ref[:, lanes].astype(BF16))
        e = jnp.exp(s - jnp.max(s, axis=-1, keepdims=True))
        den = jnp.sum(e, axis=-1, keepdims=True)
        outs.append(_dot(e.astype(BF16), kv_ref[:, vl].astype(BF16)) / den)
    o_ref[...] = jnp.concatenate(outs, axis=1).astype(o_ref.dtype)


def _mem_attention(h3, mem_kv3, q_col_block, tm):
    B, S, _ = h3.shape
    return pl.pallas_call(
        _mem_kernel,
        grid=(B, S // tm),
        in_specs=[pl.BlockSpec((None, tm, MEM_W), lambda b, i: (b, i, q_col_block)),
                  pl.BlockSpec((None, MEM_LEN, 2 * MEM_W), lambda b, i: (b, 0, 0))],
        out_specs=pl.BlockSpec((None, tm, MEM_W), lambda b, i: (b, i, 0)),
        out_shape=jax.ShapeDtypeStruct((B, S, MEM_W), BF16),
        compiler_params=_cparams(("parallel", "parallel"), 32),
        name="mem_attention",
    )(h3, mem_kv3)


def _route(lg):
    lane = lax.broadcasted_iota(jnp.int32, lg.shape, 1)
    lane_f = lane.astype(F32)
    neg = -jnp.inf
    far = float(LANES)

    lgg = jnp.where((lane >= N_EXPERTS) & (lane < N_EXPERTS + N_GROUPS), lg, neg)
    gmax = jnp.max(lgg, axis=-1, keepdims=True)
    p_top = 1.0 / jnp.sum(jnp.exp(lgg - gmax), axis=-1, keepdims=True)
    g_lane = jnp.min(jnp.where(lgg == gmax, lane_f, far), axis=-1, keepdims=True)
    first = (g_lane - float(N_EXPERTS)) * float(EXPERTS_PER_GROUP)

    le = jnp.where((lane_f >= first) & (lane_f < first + float(EXPERTS_PER_GROUP)), lg, neg)
    m1 = jnp.max(le, axis=-1, keepdims=True)
    i1 = jnp.min(jnp.where(le == m1, lane_f, far), axis=-1, keepdims=True)
    le2 = jnp.where(lane_f == i1, neg, le)
    m2 = jnp.max(le2, axis=-1, keepdims=True)
    i2 = jnp.min(jnp.where(le2 == m2, lane_f, far), axis=-1, keepdims=True)
    e2 = jnp.exp(m2 - m1)
    den = 1.0 + e2
    gates = p_top * (jnp.where(lane_f == i1, 1.0 / den, 0.0) + jnp.where(lane_f == i2, e2 / den, 0.0))
    return gates, g_lane - float(N_EXPERTS)


def _oproj_kernel(seq_ref, mem_ref, x_ref, wo_ref, g_ref, b_ref, wr_ref, br_ref, before_ref,
                  o_ref, gates_ref, drow_ref, cnt_ref, wob_ref, wrh_ref, wrl_ref):
    @pl.when(pl.program_id(0) == 0)
    def _():
        wob_ref[...] = wo_ref[...].astype(BF16)
        hi, lo = _split_bf16(wr_ref[...])
        wrh_ref[...] = hi
        wrl_ref[...] = lo

    y = (_dot(seq_ref[...], wob_ref[0:SEQ_MIX_W, :]) + _dot(mem_ref[...], wob_ref[SEQ_MIX_W:, :]))
    x1 = _layer_norm(DEEPNORM_ALPHA * x_ref[...] + y, g_ref[...], b_ref[...])
    o_ref[...] = x1
    xh, xl = _split_bf16(x1)
    wrh = wrh_ref[...]
    lg2 = _dot(xh, jnp.concatenate([wrh, wrl_ref[...]], axis=1))
    gates, grp = _route(lg2[:, :LANES] + lg2[:, LANES:] + _dot(xl, wrh) + br_ref[...])

    lane_f = lax.broadcasted_iota(jnp.int32, gates.shape, 1).astype(F32)
    onehot = jnp.where(lane_f == grp, 1.0, 0.0)
    ranks = _dot(before_ref[...], onehot.astype(BF16))
    rank = jnp.sum(jnp.where(lane_f == grp, ranks, 0.0), axis=-1, keepdims=True)
    counts = jnp.sum(onehot, axis=0, keepdims=True)
    padded = jnp.floor((counts + float(MOE_ALIGN - 1)) * (1.0 / MOE_ALIGN)) * float(MOE_ALIGN)
    offset = jnp.sum(jnp.where(lane_f < grp, padded, 0.0), axis=-1, keepdims=True)
    dest = offset + rank
    gates_ref[...] = gates + jnp.where(lane_f == float(MOE_DEST_LANE), dest, 0.0)
    cnt_ref[...] = jnp.broadcast_to(counts, cnt_ref.shape)
    hi16 = jnp.floor(dest * (1.0 / 16.0))
    digits = (jnp.where(lane_f == 0.0, hi16, 0.0) + jnp.where(lane_f == 1.0, dest - 16.0 * hi16, 0.0))
    sel_lane = lax.broadcasted_iota(jnp.int32, (8, LANES), 1)
    sel = jnp.where(sel_lane == 0, 16.0, jnp.where(sel_lane == 1, 1.0, 0.0)).astype(BF16)
    drow_ref[...] = _dot_nt(sel, digits.astype(BF16))


def _oproj_ln_route(seq, memo, x, w_o, layer, g, b, w_group, b_group, w_router, b_router):
    T = x.shape[0]
    tm = MOE_TM
    pad = LANES - N_EXPERTS - N_GROUPS
    wr = jnp.concatenate([w_router, w_group, jnp.zeros((D_MODEL, pad), F32)], axis=1)
    br = jnp.concatenate([b_router, b_group, jnp.zeros((pad,), F32)]).reshape(1, LANES)
    before = jnp.asarray(np.arange(tm)[None, :] < np.arange(tm)[:, None], dtype=BF16)
    const = lambda i: (0, 0)
    return pl.pallas_call(
        _oproj_kernel,
        grid=(T // tm,),
        in_specs=[pl.BlockSpec((tm, SEQ_MIX_W), lambda i: (i, 0)),
                  pl.BlockSpec((tm, MEM_W), lambda i: (i, 0)),
                  pl.BlockSpec((tm, D_MODEL), lambda i: (i, 0)),
                  pl.BlockSpec((None, D_MODEL, D_MODEL), lambda i: (layer, 0, 0)),
                  pl.BlockSpec((1, D_MODEL), const),
                  pl.BlockSpec((1, D_MODEL), const),
                  pl.BlockSpec((D_MODEL, LANES), const),
                  pl.BlockSpec((1, LANES), const),
                  pl.BlockSpec((tm, tm), const)],
        out_specs=[pl.BlockSpec((tm, D_MODEL), lambda i: (i, 0)),
                   pl.BlockSpec((tm, LANES), lambda i: (i, 0)),
                   pl.BlockSpec((8, tm), lambda i: (i, 0)),
                   pl.BlockSpec((8, LANES), lambda i: (i, 0))],
        out_shape=[jax.ShapeDtypeStruct((T, D_MODEL), F32),
                   jax.ShapeDtypeStruct((T, LANES), F32),
                   jax.ShapeDtypeStruct((8 * (T // tm), tm), F32),
                   jax.ShapeDtypeStruct((8 * (T // tm), LANES), F32)],
        scratch_shapes=[pltpu.VMEM((D_MODEL, D_MODEL), BF16),
                        pltpu.VMEM((D_MODEL, LANES), BF16),
                        pltpu.VMEM((D_MODEL, LANES), BF16)],
        compiler_params=_cparams(("arbitrary",), 40),
        name="oproj_ln_route",
    )(seq, memo, x, w_o, g.reshape(1, D_MODEL), b.reshape(1, D_MODEL), wr, br, before)


def _moe_kernel(npairs_ref, pstart_ref, pgrp_ref,
                x_ref, gates_ref, drow_ref, wg_ref, wu_ref, wd_ref, g_ref, b_ref, o_ref,
                wgu_s, wd_s, xs_ref, gs_ref, ys_ref):
    step = pl.program_id(0)
    tm = MOE_TM
    R = MOE_R
    nrows = MOE_SORT_ROWS

    @pl.when(step < N_EXPERTS)
    def _():
        wgu_s[step, :, 0:D_EXPERT] = wg_ref[...].astype(BF16)
        wgu_s[step, :, D_EXPERT:] = wu_ref[...].astype(BF16)
        row0 = pl.multiple_of((step % EXPERTS_PER_GROUP) * D_EXPERT, D_EXPERT)
        wd_s[step // EXPERTS_PER_GROUP, pl.ds(row0, D_EXPERT), :] = wd_ref[...].astype(BF16)

    @pl.when(step >= N_EXPERTS)
    def _():
        tile = step - N_EXPERTS
        x = x_ref[...]
        gates = gates_ref[...]
        lane_g = lax.broadcasted_iota(jnp.int32, gates.shape, 1)
        dest_col = jnp.sum(jnp.where(lane_g == MOE_DEST_LANE, gates, 0.0), axis=-1, keepdims=True)
        used = MOE_TM + N_GROUPS * MOE_ALIGN
        sorted_row = lax.broadcasted_iota(jnp.int32, (used, tm), 0).astype(F32)
        perm = jnp.where(sorted_row == drow_ref[0:1, :], 1.0, 0.0).astype(BF16)
        xs_ref[0:used, :] = _dot(perm, x.astype(BF16)).astype(BF16)
        xs_ref[used:, :] = jnp.zeros((nrows - used, D_MODEL), BF16)
        ghi, glo = _split_bf16(gates)
        g2 = _dot(perm, jnp.concatenate([ghi, glo], axis=1))
        gs_ref[0:used, :] = g2[:, :LANES] + g2[:, LANES:]
        gs_ref[used:, :] = jnp.zeros((nrows - used, LANES), F32)
        ys_ref[...] = jnp.zeros_like(ys_ref)

        def pair(p, _):
            start = pstart_ref[tile * MOE_MAX_PAIRS + p]
            grp = pgrp_ref[tile * MOE_MAX_PAIRS + p]
            rows = pl.ds(pl.multiple_of(start, MOE_ALIGN), R)
            xblk = xs_ref[rows, :]
            gblk = gs_ref[rows, :]
            lane = lax.broadcasted_iota(jnp.int32, gblk.shape, 1)
            hs = []
            for j in range(EXPERTS_PER_GROUP):
                e = grp * EXPERTS_PER_GROUP + j
                hgu = _dot(xblk, wgu_s[e])
                hg = hgu[:, :D_EXPERT]
                ge = jnp.sum(jnp.where(lane == e, gblk, 0.0), axis=-1, keepdims=True)
                hs.append((hg * jax.nn.sigmoid(hg) * hgu[:, D_EXPERT:] * ge).astype(BF16))
            ys_ref[rows, :] += _dot(jnp.concatenate(hs, axis=1), wd_s[grp])
            return 0

        lax.fori_loop(0, npairs_ref[tile], pair, 0)

        sorted_col = lax.broadcasted_iota(jnp.int32, (tm, nrows), 1).astype(F32)
        unperm = jnp.where(sorted_col == dest_col, 1.0, 0.0).astype(BF16)
        y = _dot(unperm, ys_ref[...].astype(BF16))
        o_ref[...] = _layer_norm(DEEPNORM_ALPHA * x + y, g_ref[...], b_ref[...])


def _moe_plan(counts):
    ntiles = counts.shape[0]
    max_blocks = -(-MOE_TM // MOE_R)
    padded = (counts + (MOE_ALIGN - 1)) // MOE_ALIGN * MOE_ALIGN
    seg_start = jnp.cumsum(padded, axis=1) - padded
    nblk = (counts + (MOE_R - 1)) // MOE_R
    j = jnp.arange(max_blocks, dtype=jnp.int32)[None, None, :]
    active = (j < nblk[:, :, None]).reshape(ntiles, N_GROUPS * max_blocks)
    order = jnp.argsort(jnp.logical_not(active), axis=1, stable=True)[:, :MOE_MAX_PAIRS].astype(jnp.int32)
    grp = order // max_blocks
    start = jnp.take_along_axis(seg_start, grp, axis=1) + (order % max_blocks) * MOE_R
    npairs = jnp.sum(active, axis=1).astype(jnp.int32)
    return npairs, start.astype(jnp.int32).reshape(-1), grp.reshape(-1)


def _moe_ln(x, gates, dest_rows, counts8, w_gate, w_up, w_down, layer, g, b):
    T = x.shape[0]
    tm = MOE_TM
    ntiles = T // tm
    counts = counts8.reshape(ntiles, 8, LANES)[:, 0, :N_GROUPS].astype(jnp.int32)
    npairs, pstart, pgrp = _moe_plan(counts)

    def tile_map(s, *_):
        return (jnp.maximum(s - N_EXPERTS, 0), 0)

    def w_map(s, *_):
        return (layer, jnp.minimum(s, N_EXPERTS - 1), 0, 0)

    const = lambda s, *_: (0, 0)
    grid_spec = pltpu.PrefetchScalarGridSpec(
        num_scalar_prefetch=3,
        grid=(N_EXPERTS + ntiles,),
        in_specs=[pl.BlockSpec((tm, D_MODEL), tile_map),
                  pl.BlockSpec((tm, LANES), tile_map),
                  pl.BlockSpec((8, tm), tile_map),
                  pl.BlockSpec((None, None, D_MODEL, D_EXPERT), w_map),
                  pl.BlockSpec((None, None, D_MODEL, D_EXPERT), w_map),
                  pl.BlockSpec((None, None, D_EXPERT, D_MODEL), w_map),
                  pl.BlockSpec((1, D_MODEL), const),
                  pl.BlockSpec((1, D_MODEL), const)],
        out_specs=pl.BlockSpec((tm, D_MODEL), tile_map),
        scratch_shapes=[pltpu.VMEM((N_EXPERTS, D_MODEL, 2 * D_EXPERT), BF16),
                        pltpu.VMEM((N_GROUPS, EXPERTS_PER_GROUP * D_EXPERT, D_MODEL), BF16),
                        pltpu.VMEM((MOE_SORT_ROWS, D_MODEL), BF16),
                        pltpu.VMEM((MOE_SORT_ROWS, LANES), F32),
                        pltpu.VMEM((MOE_SORT_ROWS, D_MODEL), F32)])
    return pl.pallas_call(
        _moe_kernel,
        grid_spec=grid_spec,
        out_shape=jax.ShapeDtypeStruct((T, D_MODEL), F32),
        compiler_params=_cparams(("arbitrary",), 56),
        name="moe_ln",
    )(npairs, pstart, pgrp, x, gates, dest_rows, w_gate, w_up, w_down,
      g.reshape(1, D_MODEL), b.reshape(1, D_MODEL))


def kernel(x, mem, a_w_in, a_lower_bounds, a_gnorm, b_w_in, w_kv_shared, w_mem_kv, w_o,
           ln_mix_g, ln_mix_b, ln_ffn_g, ln_ffn_b, w_group, b_group, w_router, b_router,
           w_gate, w_up, w_down):
    B, S, D = x.shape
    T = B * S
    xf = x.reshape(T, D)
    memf = mem.reshape(B * MEM_LEN, D)
    a_cols = a_w_in.shape[-1]
    b_cols = b_w_in.shape[-1]
    kv3 = None
    for layer in range(DEPTH):
        mem_kv = _matmul(memf, w_mem_kv, layer, 512, 2 * MEM_W, "mem_kv")
        mem_kv3 = mem_kv.reshape(B, MEM_LEN, 2 * MEM_W)
        if layer < N_A_LAYERS:
            h, f_raw = _in_proj_a(xf, a_w_in, layer, 256)
            h3 = h.reshape(B, S, a_cols)
            seq = _hgrn(h3, f_raw.reshape(B, S, SEQ_MIX_W), a_lower_bounds, a_gnorm[layer], layer)
            q_col_block = 4 * SEQ_MIX_W // MEM_W
        else:
            if layer == N_A_LAYERS:
                kv = _matmul(xf, w_kv_shared[None], 0, 512, 2 * SEQ_MIX_W, "kv_shared", BF16)
                kv3 = kv.reshape(B, S, 2 * SEQ_MIX_W)
            h = _matmul(xf, b_w_in, layer - N_A_LAYERS, 512, b_cols, "in_proj_b", BF16)
            h3 = h.reshape(B, S, b_cols)
            seq = _sb_attention(h3, kv3)
            q_col_block = SEQ_MIX_W // MEM_W
        memo = _mem_attention(h3, mem_kv3, q_col_block, 512)
        x1, gates, dest_rows, counts8 = _oproj_ln_route(
            seq.reshape(T, SEQ_MIX_W), memo.reshape(T, MEM_W), xf, w_o, layer,
            ln_mix_g[layer], ln_mix_b[layer],
            w_group[layer], b_group[layer], w_router[layer], b_router[layer])
        xf = _moe_ln(x1, gates, dest_rows, counts8, w_gate, w_up, w_down, layer,
                     ln_ffn_g[layer], ln_ffn_b[layer])
    return xf.reshape(B, S, D)
```

```python
import functools
import math

import numpy as np
import jax
import jax.numpy as jnp
from jax import lax
from jax.experimental import pallas as pl
from jax.experimental.pallas import tpu as pltpu

F32 = jnp.float32
BF16 = jnp.bfloat16

D_MODEL = 1024
DEPTH = 4
N_A_LAYERS = DEPTH // 2
MEM_LEN = 256
MEM_HEADS = 4
HEAD_DIM = 64
MEM_W = MEM_HEADS * HEAD_DIM
SEQ_MIX_W = D_MODEL - MEM_W
HGRN_DK = 128
HGRN_HEADS = SEQ_MIX_W // HGRN_DK
SB_HEADS = SEQ_MIX_W // HEAD_DIM
N_GROUPS = 4
EXPERTS_PER_GROUP = 4
N_EXPERTS = N_GROUPS * EXPERTS_PER_GROUP
D_EXPERT = 256
DEEPNORM_ALPHA = (2 * DEPTH) ** 0.25
LN_EPS = 1e-5
RMS_EPS = 1e-6

LANES = 128
HGRN_C = 128
SB_TK = 256
SB_SUB = 2
SB_ZERO_MASS = 110.0
MOE_TM = 512
MOE_R = 144
MOE_ALIGN = 16
MOE_MAX_PAIRS = MOE_TM // MOE_R + N_GROUPS
MOE_SORT_ROWS = -(-(MOE_TM + N_GROUPS * MOE_ALIGN + MOE_R) // 256) * 256
MOE_DEST_LANE = N_EXPERTS
OPROJ_TILES = 1
MIB = 1024 * 1024


def _cparams(semantics, vmem_mib):
    return pltpu.CompilerParams(dimension_semantics=semantics, vmem_limit_bytes=vmem_mib * MIB)


def _dot(a, b):
    return jnp.dot(a, b, preferred_element_type=F32)


def _dot_nt(a, b):
    return lax.dot_general(a, b, (((1,), (1,)), ((), ())), preferred_element_type=F32)


def _dot_tn(a, b):
    return lax.dot_general(a, b, (((0,), (0,)), ((), ())), preferred_element_type=F32)


def _split_bf16(x):
    hi = x.astype(BF16)
    lo = (x - hi.astype(F32)).astype(BF16)
    return hi, lo


def _layer_norm(r, g, b):
    mu = jnp.mean(r, axis=-1, keepdims=True)
    d = r - mu
    var = jnp.mean(d * d, axis=-1, keepdims=True)
    return d * lax.rsqrt(var + LN_EPS) * g + b


def _mm_kernel(x_ref, w_ref, o_ref, wb_ref):
    @pl.when(pl.program_id(1) == 0)
    def _():
        wb_ref[...] = w_ref[...].astype(BF16)

    o_ref[...] = _dot(x_ref[...].astype(BF16), wb_ref[...]).astype(o_ref.dtype)


def _matmul(x, w_stack, layer, tm, tn, name, out_dtype=F32):
    M, K = x.shape
    N = w_stack.shape[2]
    assert M % tm == 0 and N % tn == 0
    return pl.pallas_call(
        _mm_kernel,
        grid=(N // tn, M // tm),
        in_specs=[pl.BlockSpec((tm, K), lambda j, i: (i, 0)),
                  pl.BlockSpec((None, K, tn), lambda j, i: (layer, 0, j))],
        out_specs=pl.BlockSpec((tm, tn), lambda j, i: (i, j)),
        out_shape=jax.ShapeDtypeStruct((M, N), out_dtype),
        scratch_shapes=[pltpu.VMEM((K, tn), BF16)],
        compiler_params=_cparams(("parallel", "arbitrary"), 48),
        name=name,
    )(x, w_stack)


def _in_proj_a_kernel(x_ref, w_ref, o_ref, f_ref, wb_ref):
    @pl.when(pl.program_id(0) == 0)
    def _():
        wb_ref[...] = w_ref[...].astype(BF16)

    acc = _dot(x_ref[...].astype(BF16), wb_ref[...])
    o_ref[...] = acc.astype(BF16)
    f_ref[...] = acc[:, SEQ_MIX_W:2 * SEQ_MIX_W]


def _in_proj_a(x, w_stack, layer, tm):
    M, K = x.shape
    N = w_stack.shape[2]
    return pl.pallas_call(
        _in_proj_a_kernel,
        grid=(M // tm,),
        in_specs=[pl.BlockSpec((tm, K), lambda i: (i, 0)),
                  pl.BlockSpec((None, K, N), lambda i: (layer, 0, 0))],
        out_specs=[pl.BlockSpec((tm, N), lambda i: (i, 0)),
                   pl.BlockSpec((tm, SEQ_MIX_W), lambda i: (i, 0))],
        out_shape=[jax.ShapeDtypeStruct((M, N), BF16),
                   jax.ShapeDtypeStruct((M, SEQ_MIX_W), F32)],
        scratch_shapes=[pltpu.VMEM((K, N), BF16)],
        compiler_params=_cparams(("arbitrary",), 52),
        name="in_proj_a",
    )(x, w_stack)


def _hgrn_levels():
    m = HGRN_C // 2
    out = []
    while m >= 1:
        out.append(m)
        m //= 2
    return out


def _hgrn_level_decay(m, G, f, row):
    C = HGRN_C
    if m >= 4:
        G3 = G.reshape(C // (2 * m), 2 * m, LANES)
        return jnp.exp(-jnp.abs(G3 - G3[:, m - 1:m, :])).reshape(C, LANES)
    if m == 2:
        f_next = pltpu.roll(f, C - 1, axis=0)
        f_prev = pltpu.roll(f, 1, axis=0)
        p = row & 3
        return jnp.where(p == 0, f_next, jnp.where(p == 1, 1.0, jnp.where(p == 2, f, f * f_prev)))
    return jnp.where((row & 1) == 1, f, 1.0)


def _hgrn_kernel(layer, q_ref, f_ref, i_ref, g_ref, lbp_ref, gn_ref, incl_ref, o_ref, st_ref):
    C = HGRN_C

    @pl.when(pl.program_id(1) == 0)
    def _():
        st_ref[...] = jnp.zeros_like(st_ref)

    a = lbp_ref[...]
    ea = jnp.exp(a - jnp.max(a, axis=0, keepdims=True))
    pa = ea / jnp.sum(ea, axis=0, keepdims=True)
    lb_all = jnp.zeros((1, SEQ_MIX_W), F32)
    for l in range(1, layer + 1):
        lb_all = lb_all + pa[l:l + 1]

    t_idx = lax.broadcasted_iota(jnp.int32, (C, C), 0)
    s_idx = lax.broadcasted_iota(jnp.int32, (C, C), 1)
    split = t_idx ^ s_idx
    levels = _hgrn_levels()
    incl = incl_ref[...]
    row = lax.broadcasted_iota(jnp.int32, (C, LANES), 0)

    for h in range(HGRN_HEADS):
        lanes = slice(h * LANES, (h + 1) * LANES)
        lb = lb_all[:, lanes]
        fr = f_ref[:, lanes]
        logsig = jnp.minimum(fr, 0.0) - jnp.log(1.0 + jnp.exp(-jnp.abs(fr)))
        la = jnp.log(lb)
        lc = jnp.log1p(-lb) + logsig
        logf = jnp.maximum(la, lc) + jnp.log(1.0 + jnp.exp(-jnp.abs(la - lc)))
        f = jnp.exp(logf)
        k = 1.0 - f

        hi = logf.astype(BF16)
        r1 = logf - hi.astype(F32)
        mid = r1.astype(BF16)
        lo = (r1 - mid.astype(F32)).astype(BF16)
        g3 = _dot(incl, jnp.concatenate([hi, mid, lo], axis=1))
        G = g3[:, :LANES] + g3[:, LANES:2 * LANES] + g3[:, 2 * LANES:]
        dec_g = jnp.exp(G)
        dec_suf = jnp.exp(G[C - 1:C, :] - G)

        qr = q_ref[:, lanes].astype(F32)
        q = qr * jax.nn.sigmoid(qr) * (HGRN_DK ** -0.5)
        v = i_ref[:, lanes].astype(BF16)

        sc = None
        for li in range(len(levels) - 1, -1, -1):
            m = levels[li]
            u = (jnp.where((row & m) != 0, q, k) * _hgrn_level_decay(m, G, f, row)).astype(BF16)
            p = _dot_nt(u, u)
            sc = p if sc is None else jnp.where(split >= m, p, sc)
        sc = jnp.where(s_idx < t_idx, sc, 0.0)
        sc = jnp.where(s_idx == t_idx, jnp.sum(q * k, axis=-1, keepdims=True), sc)

        st = st_ref[h]
        o = _dot(sc.astype(BF16), v) + _dot_nt((q * dec_g).astype(BF16), st.astype(BF16))
        st_ref[h] = st * dec_g[C - 1:C, :] + _dot_tn(v, (k * dec_suf).astype(BF16))

        o = o * lax.rsqrt(jnp.mean(o * o, axis=-1, keepdims=True) + RMS_EPS)
        gr = g_ref[:, lanes].astype(F32)
        o_ref[:, lanes] = (o * gn_ref[:, lanes] * (gr * jax.nn.sigmoid(gr))).astype(o_ref.dtype)


def _hgrn(h3, f3, a_lower_bounds, gnorm, layer):
    B, S, _ = h3.shape
    C = HGRN_C
    W = SEQ_MIX_W
    incl = jnp.asarray(np.arange(C)[None, :] <= np.arange(C)[:, None], dtype=BF16)

    def col(j):
        return pl.BlockSpec((None, C, W), lambda b, c: (b, c, j))

    return pl.pallas_call(
        functools.partial(_hgrn_kernel, layer),
        grid=(B, S // C),
        in_specs=[col(0), col(0), col(2), col(3),
                  pl.BlockSpec((N_A_LAYERS, W), lambda b, c: (0, 0)),
                  pl.BlockSpec((1, W), lambda b, c: (0, 0)),
                  pl.BlockSpec((C, C), lambda b, c: (0, 0))],
        out_specs=pl.BlockSpec((None, C, W), lambda b, c: (b, c, 0)),
        out_shape=jax.ShapeDtypeStruct((B, S, W), BF16),
        scratch_shapes=[pltpu.VMEM((HGRN_HEADS, LANES, HGRN_DK), F32)],
        compiler_params=_cparams(("parallel", "arbitrary"), 32),
        name="hgrn2",
    )(h3, f3, h3, h3, a_lower_bounds, gnorm.reshape(1, W), incl)


def _sb_kernel(q_ref, k_ref, v_ref, later_ref, o_ref, acc_ref, carry_ref):
    TK = SB_TK
    TQ = SB_SUB * TK
    qi = pl.program_id(2)
    base = SB_SUB * qi
    lane = lax.broadcasted_iota(jnp.int32, (TQ, LANES), 1)
    q = q_ref[...] * (1.0 / math.sqrt(HEAD_DIM))
    zero = jnp.zeros_like(q)
    qh = [jnp.where(lane < HEAD_DIM, q, zero), jnp.where(lane >= HEAD_DIM, q, zero)]
    later = later_ref[...]
    row = lax.broadcasted_iota(jnp.int32, (TK, TK), 0)
    col = lax.broadcasted_iota(jnp.int32, (TK, TK), 1)
    causal = col < row

    def sweep(h, rows, kb, masked, carry):
        start = pl.multiple_of(kb * TK, TK)
        z = _dot_nt(qh[h][rows], k_ref[pl.ds(start, TK), :])
        sp = jnp.maximum(z, 0.0) + jnp.log(1.0 + jnp.exp(-jnp.abs(z)))
        spm = jnp.where(causal, sp, 0.0) if masked else sp
        rest = _dot(spm.astype(BF16), later) + carry
        a = jnp.exp(z - sp - rest)
        if masked:
            a = jnp.where(causal, a, 0.0)
        contrib = _dot(a.astype(BF16), v_ref[pl.ds(start, TK), :])
        return contrib, carry + jnp.sum(spm, axis=1, keepdims=True)

    subs = [slice(r * TK, (r + 1) * TK) for r in range(SB_SUB)]
    no_mass = jnp.zeros((TK, 1), F32)
    has_prev = qi > 0
    for h in range(2):
        for r, rows in enumerate(subs):
            acc_d, c_d = sweep(h, rows, base + r, True, no_mass)
            acc_p, c_p = sweep(h, rows, jnp.maximum(base + r - 1, 0), False, c_d)
            if r == 0:
                acc_p = jnp.where(has_prev, acc_p, 0.0)
                c_p = jnp.where(has_prev, c_p, c_d)
            acc_ref[h, rows, :] = acc_d + acc_p
            carry_ref[h, rows, :] = c_p

    def min_mass():
        return jnp.minimum(jnp.min(carry_ref[0]), jnp.min(carry_ref[1]))

    def more(state):
        j, mass = state
        return jnp.logical_and(j < base + SB_SUB - 2, mass < SB_ZERO_MASS)

    def earlier(state):
        j, _ = state
        for h in range(2):
            for r, rows in enumerate(subs):
                kb = base + r - 2 - j
                live = kb >= 0
                contrib, c = sweep(h, rows, jnp.maximum(kb, 0), False, carry_ref[h, rows, :])
                acc_ref[h, rows, :] += jnp.where(live, contrib, 0.0)
                carry_ref[h, rows, :] = jnp.where(live, c, carry_ref[h, rows, :])
        return j + 1, min_mass()

    lax.while_loop(more, earlier, (jnp.int32(0), min_mass()))
    o_ref[...] = jnp.where(lane < HEAD_DIM, acc_ref[0], acc_ref[1]).astype(o_ref.dtype)


def _sb_attention(h3, kv3):
    B, S, _ = h3.shape
    TK = SB_TK
    TQ = SB_SUB * TK
    pairs = SEQ_MIX_W // LANES
    later = jnp.asarray(np.arange(TK)[:, None] > np.arange(TK)[None, :], dtype=BF16)
    return pl.pallas_call(
        _sb_kernel,
        grid=(B, pairs, S // TQ),
        in_specs=[pl.BlockSpec((None, TQ, LANES), lambda b, p, i: (b, i, p)),
                  pl.BlockSpec((None, S, LANES), lambda b, p, i: (b, 0, p)),
                  pl.BlockSpec((None, S, LANES), lambda b, p, i: (b, 0, pairs + p)),
                  pl.BlockSpec((TK, TK), lambda b, p, i: (0, 0))],
        out_specs=pl.BlockSpec((None, TQ, LANES), lambda b, p, i: (b, i, p)),
        out_shape=jax.ShapeDtypeStruct((B, S, SEQ_MIX_W), BF16),
        scratch_shapes=[pltpu.VMEM((2, TQ, LANES), F32), pltpu.VMEM((2, TQ, 1), F32)],
        compiler_params=_cparams(("parallel", "parallel", "arbitrary"), 32),
        name="stick_breaking",
    )(h3, kv3, kv3, later)


def _mem_kernel(q_ref, kv_ref, o_ref):
    q = q_ref[...] * (1.0 / math.sqrt(HEAD_DIM))
    outs = []
    for h in range(MEM_HEADS):
        lanes = slice(h * HEAD_DIM, (h + 1) * HEAD_DIM)
        vl = slice(MEM_W + h * HEAD_DIM, MEM_W + (h + 1) * HEAD_DIM)
        s = _dot_nt(q[:, lanes].astype(BF16), kv_ref[:, lanes].astype(BF16))
        e = jnp.exp(s - jnp.max(s, axis=-1, keepdims=True))
        den = jnp.sum(e, axis=-1, keepdims=True)
        outs.append(_dot(e.astype(BF16), kv_ref[:, vl].astype(BF16)) / den)
    o_ref[...] = jnp.concatenate(outs, axis=1).astype(o_ref.dtype)


def _mem_attention(h3, mem_kv3, q_col_block, tm):
    B, S, _ = h3.shape
    return pl.pallas_call(
        _mem_kernel,
        grid=(B, S // tm),
        in_specs=[pl.BlockSpec((None, tm, MEM_W), lambda b, i: (b, i, q_col_block)),
                  pl.BlockSpec((None, MEM_LEN, 2 * MEM_W), lambda b, i: (b, 0, 0))],
        out_specs=pl.BlockSpec((None, tm, MEM_W), lambda b, i: (b, i, 0)),
        out_shape=jax.ShapeDtypeStruct((B, S, MEM_W), BF16),
        compiler_params=_cparams(("parallel", "parallel"), 32),
        name="mem_attention",
    )(h3, mem_kv3)


def _route(lg):
    lane = lax.broadcasted_iota(jnp.int32, lg.shape, 1)
    lane_f = lane.astype(F32)
    neg = -jnp.inf
    far = float(LANES)

    lgg = jnp.where((lane >= N_EXPERTS) & (lane < N_EXPERTS + N_GROUPS), lg, neg)
    gmax = jnp.max(lgg, axis=-1, keepdims=True)
    p_top = 1.0 / jnp.sum(jnp.exp(lgg - gmax), axis=-1, keepdims=True)
    g_lane = jnp.min(jnp.where(lgg == gmax, lane_f, far), axis=-1, keepdims=True)
    first = (g_lane - float(N_EXPERTS)) * float(EXPERTS_PER_GROUP)

    le = jnp.where((lane_f >= first) & (lane_f < first + float(EXPERTS_PER_GROUP)), lg, neg)
    m1 = jnp.max(le, axis=-1, keepdims=True)
    i1 = jnp.min(jnp.where(le == m1, lane_f, far), axis=-1, keepdims=True)
    le2 = jnp.where(lane_f == i1, neg, le)
    m2 = jnp.max(le2, axis=-1, keepdims=True)
    i2 = jnp.min(jnp.where(le2 == m2, lane_f, far), axis=-1, keepdims=True)
    e2 = jnp.exp(m2 - m1)
    den = 1.0 + e2
    gates = p_top * (jnp.where(lane_f == i1, 1.0 / den, 0.0) + jnp.where(lane_f == i2, e2 / den, 0.0))
    return gates, g_lane - float(N_EXPERTS)


def _oproj_kernel(seq_ref, mem_ref, x_ref, wo_ref, g_ref, b_ref, wr_ref, br_ref, before_ref,
                  o_ref, gates_ref, drow_ref, cnt_ref, wob_ref, wrh_ref, wrl_ref):
    @pl.when(pl.program_id(0) == 0)
    def _():
        wob_ref[...] = wo_ref[...].astype(BF16)
        hi, lo = _split_bf16(wr_ref[...])
        wrh_ref[...] = hi
        wrl_ref[...] = lo

    for t in range(OPROJ_TILES):
        rows = slice(t * MOE_TM, (t + 1) * MOE_TM)
        meta = slice(8 * t, 8 * (t + 1))
        y = (_dot(seq_ref[rows, :], wob_ref[0:SEQ_MIX_W, :])
             + _dot(mem_ref[rows, :], wob_ref[SEQ_MIX_W:, :]))
        x1 = _layer_norm(DEEPNORM_ALPHA * x_ref[rows, :] + y, g_ref[...], b_ref[...])
        o_ref[rows, :] = x1
        xh, xl = _split_bf16(x1)
        wrh = wrh_ref[...]
        lg2 = _dot(xh, jnp.concatenate([wrh, wrl_ref[...]], axis=1))
        gates, grp = _route(lg2[:, :LANES] + lg2[:, LANES:] + _dot(xl, wrh) + br_ref[...])

        lane_f = lax.broadcasted_iota(jnp.int32, gates.shape, 1).astype(F32)
        onehot = jnp.where(lane_f == grp, 1.0, 0.0)
        ranks = _dot(before_ref[...], onehot.astype(BF16))
        rank = jnp.sum(jnp.where(lane_f == grp, ranks, 0.0), axis=-1, keepdims=True)
        counts = jnp.sum(onehot, axis=0, keepdims=True)
        padded = jnp.floor((counts + float(MOE_ALIGN - 1)) * (1.0 / MOE_ALIGN)) * float(MOE_ALIGN)
        offset = jnp.sum(jnp.where(lane_f < grp, padded, 0.0), axis=-1, keepdims=True)
        dest = offset + rank
        gates_ref[rows, :] = gates + jnp.where(lane_f == float(MOE_DEST_LANE), dest, 0.0)
        cnt_ref[meta, :] = jnp.broadcast_to(counts, (8, LANES))
        hi16 = jnp.floor(dest * (1.0 / 16.0))
        digits = (jnp.where(lane_f == 0.0, hi16, 0.0)
                  + jnp.where(lane_f == 1.0, dest - 16.0 * hi16, 0.0))
        sel_lane = lax.broadcasted_iota(jnp.int32, (8, LANES), 1)
        sel = jnp.where(sel_lane == 0, 16.0, jnp.where(sel_lane == 1, 1.0, 0.0)).astype(BF16)
        drow_ref[meta, :] = _dot_nt(sel, digits.astype(BF16))


def _oproj_ln_route(seq, memo, x, w_o, layer, g, b, w_group, b_group, w_router, b_router):
    T = x.shape[0]
    tm = MOE_TM
    step_rows = OPROJ_TILES * tm
    pad = LANES - N_EXPERTS - N_GROUPS
    wr = jnp.concatenate([w_router, w_group, jnp.zeros((D_MODEL, pad), F32)], axis=1)
    br = jnp.concatenate([b_router, b_group, jnp.zeros((pad,), F32)]).reshape(1, LANES)
    before = jnp.asarray(np.arange(tm)[None, :] < np.arange(tm)[:, None], dtype=BF16)
    const = lambda i: (0, 0)
    return pl.pallas_call(
        _oproj_kernel,
        grid=(T // step_rows,),
        in_specs=[pl.BlockSpec((step_rows, SEQ_MIX_W), lambda i: (i, 0)),
                  pl.BlockSpec((step_rows, MEM_W), lambda i: (i, 0)),
                  pl.BlockSpec((step_rows, D_MODEL), lambda i: (i, 0)),
                  pl.BlockSpec((None, D_MODEL, D_MODEL), lambda i: (layer, 0, 0)),
                  pl.BlockSpec((1, D_MODEL), const),
                  pl.BlockSpec((1, D_MODEL), const),
                  pl.BlockSpec((D_MODEL, LANES), const),
                  pl.BlockSpec((1, LANES), const),
                  pl.BlockSpec((tm, tm), const)],
        out_specs=[pl.BlockSpec((step_rows, D_MODEL), lambda i: (i, 0)),
                   pl.BlockSpec((step_rows, LANES), lambda i: (i, 0)),
                   pl.BlockSpec((8 * OPROJ_TILES, tm), lambda i: (i, 0)),
                   pl.BlockSpec((8 * OPROJ_TILES, LANES), lambda i: (i, 0))],
        out_shape=[jax.ShapeDtypeStruct((T, D_MODEL), F32),
                   jax.ShapeDtypeStruct((T, LANES), F32),
                   jax.ShapeDtypeStruct((8 * (T // tm), tm), F32),
                   jax.ShapeDtypeStruct((8 * (T // tm), LANES), F32)],
        scratch_shapes=[pltpu.VMEM((D_MODEL, D_MODEL), BF16),
                        pltpu.VMEM((D_MODEL, LANES), BF16),
                        pltpu.VMEM((D_MODEL, LANES), BF16)],
        compiler_params=_cparams(("arbitrary",), 48),
        name="oproj_ln_route",
    )(seq, memo, x, w_o, g.reshape(1, D_MODEL), b.reshape(1, D_MODEL), wr, br, before)


def _moe_kernel(npairs_ref, pstart_ref, pgrp_ref,
                x_ref, gates_ref, drow_ref, wg_ref, wu_ref, wd_ref, g_ref, b_ref, o_ref,
                wgu_s, wd_s, xs_ref, gs_ref, ys_ref):
    step = pl.program_id(0)
    tm = MOE_TM
    R = MOE_R
    nrows = MOE_SORT_ROWS

    @pl.when(step < N_EXPERTS)
    def _():
        wgu_s[step, :, 0:D_EXPERT] = wg_ref[...].astype(BF16)
        wgu_s[step, :, D_EXPERT:] = wu_ref[...].astype(BF16)
        row0 = pl.multiple_of((step % EXPERTS_PER_GROUP) * D_EXPERT, D_EXPERT)
        wd_s[step // EXPERTS_PER_GROUP, pl.ds(row0, D_EXPERT), :] = wd_ref[...].astype(BF16)

    @pl.when(step >= N_EXPERTS)
    def _():
        tile = step - N_EXPERTS
        x = x_ref[...]
        gates = gates_ref[...]
        lane_g = lax.broadcasted_iota(jnp.int32, gates.shape, 1)
        dest_col = jnp.sum(jnp.where(lane_g == MOE_DEST_LANE, gates, 0.0), axis=-1, keepdims=True)
        used = MOE_TM + N_GROUPS * MOE_ALIGN
        sorted_row = lax.broadcasted_iota(jnp.int32, (used, tm), 0).astype(F32)
        perm = jnp.where(sorted_row == drow_ref[0:1, :], 1.0, 0.0).astype(BF16)
        xs_ref[0:used, :] = _dot(perm, x.astype(BF16)).astype(BF16)
        xs_ref[used:, :] = jnp.zeros((nrows - used, D_MODEL), BF16)
        ghi, glo = _split_bf16(gates)
        g2 = _dot(perm, jnp.concatenate([ghi, glo], axis=1))
        gs_ref[0:used, :] = g2[:, :LANES] + g2[:, LANES:]
        gs_ref[used:, :] = jnp.zeros((nrows - used, LANES), F32)
        ys_ref[...] = jnp.zeros_like(ys_ref)

        def pair(p, _):
            start = pstart_ref[tile * MOE_MAX_PAIRS + p]
            grp = pgrp_ref[tile * MOE_MAX_PAIRS + p]
            rows = pl.ds(pl.multiple_of(start, MOE_ALIGN), R)
            xblk = xs_ref[rows, :]
            gblk = gs_ref[rows, :]
            lane = lax.broadcasted_iota(jnp.int32, gblk.shape, 1)
            hs = []
            for j in range(EXPERTS_PER_GROUP):
                e = grp * EXPERTS_PER_GROUP + j
                hgu = _dot(xblk, wgu_s[e])
                hg = hgu[:, :D_EXPERT]
                ge = jnp.sum(jnp.where(lane == e, gblk, 0.0), axis=-1, keepdims=True)
                hs.append((hg * jax.nn.sigmoid(hg) * hgu[:, D_EXPERT:] * ge).astype(BF16))
            ys_ref[rows, :] += _dot(jnp.concatenate(hs, axis=1), wd_s[grp])
            return 0

        lax.fori_loop(0, npairs_ref[tile], pair, 0)

        sorted_col = lax.broadcasted_iota(jnp.int32, (tm, nrows), 1).astype(F32)
        unperm = jnp.where(sorted_col == dest_col, 1.0, 0.0).astype(BF16)
        y = _dot(unperm, ys_ref[...].astype(BF16))
        o_ref[...] = _layer_norm(DEEPNORM_ALPHA * x + y, g_ref[...], b_ref[...])


def _moe_plan(counts):
    ntiles = counts.shape[0]
    max_blocks = -(-MOE_TM // MOE_R)
    padded = (counts + (MOE_ALIGN - 1)) // MOE_ALIGN * MOE_ALIGN
    seg_start = jnp.cumsum(padded, axis=1) - padded
    nblk = (counts + (MOE_R - 1)) // MOE_R
    j = jnp.arange(max_blocks, dtype=jnp.int32)[None, None, :]
    active = (j < nblk[:, :, None]).reshape(ntiles, N_GROUPS * max_blocks)
    order = jnp.argsort(jnp.logical_not(active), axis=1, stable=True)[:, :MOE_MAX_PAIRS].astype(jnp.int32)
    grp = order // max_blocks
    start = jnp.take_along_axis(seg_start, grp, axis=1) + (order % max_blocks) * MOE_R
    npairs = jnp.sum(active, axis=1).astype(jnp.int32)
    return npairs, start.astype(jnp.int32).reshape(-1), grp.reshape(-1)


def _moe_ln(x, gates, dest_rows, counts8, w_gate, w_up, w_down, layer, g, b):
    T = x.shape[0]
    tm = MOE_TM
    ntiles = T // tm
    counts = counts8.reshape(ntiles, 8, LANES)[:, 0, :N_GROUPS].astype(jnp.int32)
    npairs, pstart, pgrp = _moe_plan(counts)

    def tile_map(s, *_):
        return (jnp.maximum(s - N_EXPERTS, 0), 0)

    def w_map(s, *_):
        return (layer, jnp.minimum(s, N_EXPERTS - 1), 0, 0)

    const = lambda s, *_: (0, 0)
    grid_spec = pltpu.PrefetchScalarGridSpec(
        num_scalar_prefetch=3,
        grid=(N_EXPERTS + ntiles,),
        in_specs=[pl.BlockSpec((tm, D_MODEL), tile_map),
                  pl.BlockSpec((tm, LANES), tile_map),
                  pl.BlockSpec((8, tm), tile_map),
                  pl.BlockSpec((None, None, D_MODEL, D_EXPERT), w_map),
                  pl.BlockSpec((None, None, D_MODEL, D_EXPERT), w_map),
                  pl.BlockSpec((None, None, D_EXPERT, D_MODEL), w_map),
                  pl.BlockSpec((1, D_MODEL), const),
                  pl.BlockSpec((1, D_MODEL), const)],
        out_specs=pl.BlockSpec((tm, D_MODEL), tile_map),
        scratch_shapes=[pltpu.VMEM((N_EXPERTS, D_MODEL, 2 * D_EXPERT), BF16),
                        pltpu.VMEM((N_GROUPS, EXPERTS_PER_GROUP * D_EXPERT, D_MODEL), BF16),
                        pltpu.VMEM((MOE_SORT_ROWS, D_MODEL), BF16),
                        pltpu.VMEM((MOE_SORT_ROWS, LANES), F32),
                        pltpu.VMEM((MOE_SORT_ROWS, D_MODEL), F32)])
    return pl.pallas_call(
        _moe_kernel,
        grid_spec=grid_spec,
        out_shape=jax.ShapeDtypeStruct((T, D_MODEL), F32),
        compiler_params=_cparams(("arbitrary",), 56),
        name="moe_ln",
    )(npairs, pstart, pgrp, x, gates, dest_rows, w_gate, w_up, w_down,
      g.reshape(1, D_MODEL), b.reshape(1, D_MODEL))


def kernel(x, mem, a_w_in, a_lower_bounds, a_gnorm, b_w_in, w_kv_shared, w_mem_kv, w_o,
           ln_mix_g, ln_mix_b, ln_ffn_g, ln_ffn_b, w_group, b_group, w_router, b_router,
           w_gate, w_up, w_down):
    B, S, D = x.shape
    T = B * S
    xf = x.reshape(T, D)
    memf = mem.reshape(B * MEM_LEN, D)
    a_cols = a_w_in.shape[-1]
    b_cols = b_w_in.shape[-1]
    kv3 = None
    for layer in range(DEPTH):
        mem_kv = _matmul(memf, w_mem_kv, layer, 512, 2 * MEM_W, "mem_kv")
        mem_kv3 = mem_kv.reshape(B, MEM_LEN, 2 * MEM_W)
        if layer < N_A_LAYERS:
            h, f_raw = _in_proj_a(xf, a_w_in, layer, 256)
            h3 = h.reshape(B, S, a_cols)
            seq = _hgrn(h3, f_raw.reshape(B, S, SEQ_MIX_W), a_lower_bounds, a_gnorm[layer], layer)
            q_col_block = 4 * SEQ_MIX_W // MEM_W
        else:
            if layer == N_A_LAYERS:
                kv = _matmul(xf, w_kv_shared[None], 0, 512, 2 * SEQ_MIX_W, "kv_shared", BF16)
                kv3 = kv.reshape(B, S, 2 * SEQ_MIX_W)
            h = _matmul(xf, b_w_in, layer - N_A_LAYERS, 512, b_cols, "in_proj_b", BF16)
            h3 = h.reshape(B, S, b_cols)
            seq = _sb_attention(h3, kv3)
            q_col_block = SEQ_MIX_W // MEM_W
        memo = _mem_attention(h3, mem_kv3, q_col_block, 2048)
        x1, gates, dest_rows, counts8 = _oproj_ln_route(
            seq.reshape(T, SEQ_MIX_W), memo.reshape(T, MEM_W), xf, w_o, layer,
            ln_mix_g[layer], ln_mix_b[layer],
            w_group[layer], b_group[layer], w_router[layer], b_router[layer])
        xf = _moe_ln(x1, gates, dest_rows, counts8, w_gate, w_up, w_down, layer,
                     ln_ffn_g[layer], ln_ffn_b[layer])
    return xf.reshape(B, S, D)
```

```python
import functools
import math

import numpy as np
import jax
import jax.numpy as jnp
from jax import lax
from jax.experimental import pallas as pl
from jax.experimental.pallas import tpu as pltpu

F32 = jnp.float32
BF16 = jnp.bfloat16

D_MODEL = 1024
DEPTH = 4
N_A_LAYERS = DEPTH // 2
MEM_LEN = 256
MEM_HEADS = 4
HEAD_DIM = 64
MEM_W = MEM_HEADS * HEAD_DIM
SEQ_MIX_W = D_MODEL - MEM_W
HGRN_DK = 128
HGRN_HEADS = SEQ_MIX_W // HGRN_DK
SB_HEADS = SEQ_MIX_W // HEAD_DIM
N_GROUPS = 4
EXPERTS_PER_GROUP = 4
N_EXPERTS = N_GROUPS * EXPERTS_PER_GROUP
D_EXPERT = 256
DEEPNORM_ALPHA = (2 * DEPTH) ** 0.25
LN_EPS = 1e-5
RMS_EPS = 1e-6

LANES = 128
HGRN_C = 128
SB_TK = 256
SB_ZERO_MASS = 110.0
MOE_TM = 512
MOE_R = 144
MOE_ALIGN = 16
MOE_MAX_PAIRS = MOE_TM // MOE_R + N_GROUPS
MOE_SORT_ROWS = -(-(MOE_TM + N_GROUPS * MOE_ALIGN + MOE_R) // 256) * 256
MOE_DEST_LANE = N_EXPERTS
OPROJ_TILES = 1
MIB = 1024 * 1024


def _cparams(semantics, vmem_mib):
    return pltpu.CompilerParams(dimension_semantics=semantics, vmem_limit_bytes=vmem_mib * MIB)


def _dot(a, b):
    return jnp.dot(a, b, preferred_element_type=F32)


def _dot_nt(a, b):
    return lax.dot_general(a, b, (((1,), (1,)), ((), ())), preferred_element_type=F32)


def _dot_tn(a, b):
    return lax.dot_general(a, b, (((0,), (0,)), ((), ())), preferred_element_type=F32)


def _split_bf16(x):
    hi = x.astype(BF16)
    lo = (x - hi.astype(F32)).astype(BF16)
    return hi, lo


def _layer_norm(r, g, b):
    mu = jnp.mean(r, axis=-1, keepdims=True)
    d = r - mu
    var = jnp.mean(d * d, axis=-1, keepdims=True)
    return d * lax.rsqrt(var + LN_EPS) * g + b


def _mm_kernel(x_ref, w_ref, o_ref, wb_ref):
    @pl.when(pl.program_id(1) == 0)
    def _():
        wb_ref[...] = w_ref[...].astype(BF16)

    o_ref[...] = _dot(x_ref[...].astype(BF16), wb_ref[...]).astype(o_ref.dtype)


def _matmul(x, w_stack, layer, tm, tn, name, out_dtype=F32):
    M, K = x.shape
    N = w_stack.shape[2]
    assert M % tm == 0 and N % tn == 0
    return pl.pallas_call(
        _mm_kernel,
        grid=(N // tn, M // tm),
        in_specs=[pl.BlockSpec((tm, K), lambda j, i: (i, 0)),
                  pl.BlockSpec((None, K, tn), lambda j, i: (layer, 0, j))],
        out_specs=pl.BlockSpec((tm, tn), lambda j, i: (i, j)),
        out_shape=jax.ShapeDtypeStruct((M, N), out_dtype),
        scratch_shapes=[pltpu.VMEM((K, tn), BF16)],
        compiler_params=_cparams(("parallel", "arbitrary"), 48),
        name=name,
    )(x, w_stack)


def _in_proj_a_kernel(x_ref, w_ref, o_ref, f_ref, wb_ref):
    @pl.when(pl.program_id(0) == 0)
    def _():
        wb_ref[...] = w_ref[...].astype(BF16)

    acc = _dot(x_ref[...].astype(BF16), wb_ref[...])
    o_ref[...] = acc.astype(BF16)
    f_ref[...] = acc[:, SEQ_MIX_W:2 * SEQ_MIX_W]


def _in_proj_a(x, w_stack, layer, tm):
    M, K = x.shape
    N = w_stack.shape[2]
    return pl.pallas_call(
        _in_proj_a_kernel,
        grid=(M // tm,),
        in_specs=[pl.BlockSpec((tm, K), lambda i: (i, 0)),
                  pl.BlockSpec((None, K, N), lambda i: (layer, 0, 0))],
        out_specs=[pl.BlockSpec((tm, N), lambda i: (i, 0)),
                   pl.BlockSpec((tm, SEQ_MIX_W), lambda i: (i, 0))],
        out_shape=[jax.ShapeDtypeStruct((M, N), BF16),
                   jax.ShapeDtypeStruct((M, SEQ_MIX_W), F32)],
        scratch_shapes=[pltpu.VMEM((K, N), BF16)],
        compiler_params=_cparams(("arbitrary",), 52),
        name="in_proj_a",
    )(x, w_stack)


def _hgrn_levels():
    m = HGRN_C // 2
    out = []
    while m >= 1:
        out.append(m)
        m //= 2
    return out


def _hgrn_level_decay(m, G, f, row):
    C = HGRN_C
    if m >= 4:
        G3 = G.reshape(C // (2 * m), 2 * m, LANES)
        return jnp.exp(-jnp.abs(G3 - G3[:, m - 1:m, :])).reshape(C, LANES)
    if m == 2:
        f_next = pltpu.roll(f, C - 1, axis=0)
        f_prev = pltpu.roll(f, 1, axis=0)
        p = row & 3
        return jnp.where(p == 0, f_next, jnp.where(p == 1, 1.0, jnp.where(p == 2, f, f * f_prev)))
    return jnp.where((row & 1) == 1, f, 1.0)


def _hgrn_kernel(layer, q_ref, f_ref, i_ref, g_ref, lbp_ref, gn_ref, incl_ref, o_ref, st_ref):
    C = HGRN_C

    @pl.when(pl.program_id(1) == 0)
    def _():
        st_ref[...] = jnp.zeros_like(st_ref)

    a = lbp_ref[...]
    ea = jnp.exp(a - jnp.max(a, axis=0, keepdims=True))
    pa = ea / jnp.sum(ea, axis=0, keepdims=True)
    lb_all = jnp.zeros((1, SEQ_MIX_W), F32)
    for l in range(1, layer + 1):
        lb_all = lb_all + pa[l:l + 1]

    t_idx = lax.broadcasted_iota(jnp.int32, (C, C), 0)
    s_idx = lax.broadcasted_iota(jnp.int32, (C, C), 1)
    split = t_idx ^ s_idx
    levels = _hgrn_levels()
    incl = incl_ref[...]
    row = lax.broadcasted_iota(jnp.int32, (C, LANES), 0)

    for h in range(HGRN_HEADS):
        lanes = slice(h * LANES, (h + 1) * LANES)
        lb = lb_all[:, lanes]
        fr = f_ref[:, lanes]
        logsig = jnp.minimum(fr, 0.0) - jnp.log(1.0 + jnp.exp(-jnp.abs(fr)))
        la = jnp.log(lb)
        lc = jnp.log1p(-lb) + logsig
        logf = jnp.maximum(la, lc) + jnp.log(1.0 + jnp.exp(-jnp.abs(la - lc)))
        f = jnp.exp(logf)
        k = 1.0 - f

        hi = logf.astype(BF16)
        r1 = logf - hi.astype(F32)
        mid = r1.astype(BF16)
        lo = (r1 - mid.astype(F32)).astype(BF16)
        g3 = _dot(incl, jnp.concatenate([hi, mid, lo], axis=1))
        G = g3[:, :LANES] + g3[:, LANES:2 * LANES] + g3[:, 2 * LANES:]
        dec_g = jnp.exp(G)
        dec_suf = jnp.exp(G[C - 1:C, :] - G)

        qr = q_ref[:, lanes].astype(F32)
        q = qr * jax.nn.sigmoid(qr) * (HGRN_DK ** -0.5)
        v = i_ref[:, lanes].astype(BF16)

        sc = None
        for li in range(len(levels) - 1, -1, -1):
            m = levels[li]
            u = (jnp.where((row & m) != 0, q, k) * _hgrn_level_decay(m, G, f, row)).astype(BF16)
            p = _dot_nt(u, u)
            sc = p if sc is None else jnp.where(split >= m, p, sc)
        sc = jnp.where(s_idx < t_idx, sc, 0.0)
        sc = jnp.where(s_idx == t_idx, jnp.sum(q * k, axis=-1, keepdims=True), sc)

        st = st_ref[h]
        o = _dot(sc.astype(BF16), v) + _dot_nt((q * dec_g).astype(BF16), st.astype(BF16))
        st_ref[h] = st * dec_g[C - 1:C, :] + _dot_tn(v, (k * dec_suf).astype(BF16))

        o = o * lax.rsqrt(jnp.mean(o * o, axis=-1, keepdims=True) + RMS_EPS)
        gr = g_ref[:, lanes].astype(F32)
        o_ref[:, lanes] = (o * gn_ref[:, lanes] * (gr * jax.nn.sigmoid(gr))).astype(o_ref.dtype)


def _hgrn(h3, f3, a_lower_bounds, gnorm, layer):
    B, S, _ = h3.shape
    C = HGRN_C
    W = SEQ_MIX_W
    incl = jnp.asarray(np.arange(C)[None, :] <= np.arange(C)[:, None], dtype=BF16)

    def col(j):
        return pl.BlockSpec((None, C, W), lambda b, c: (b, c, j))

    return pl.pallas_call(
        functools.partial(_hgrn_kernel, layer),
        grid=(B, S // C),
        in_specs=[col(0), col(0), col(2), col(3),
                  pl.BlockSpec((N_A_LAYERS, W), lambda b, c: (0, 0)),
                  pl.BlockSpec((1, W), lambda b, c: (0, 0)),
                  pl.BlockSpec((C, C), lambda b, c: (0, 0))],
        out_specs=pl.BlockSpec((None, C, W), lambda b, c: (b, c, 0)),
        out_shape=jax.ShapeDtypeStruct((B, S, W), BF16),
        scratch_shapes=[pltpu.VMEM((HGRN_HEADS, LANES, HGRN_DK), F32)],
        compiler_params=_cparams(("parallel", "arbitrary"), 32),
        name="hgrn2",
    )(h3, f3, h3, h3, a_lower_bounds, gnorm.reshape(1, W), incl)


def _sb_kernel(q_ref, k_ref, v_ref, later_ref, o_ref, acc_ref, carry_ref):
    TK = SB_TK
    TQ = 2 * TK
    qi = pl.program_id(2)
    lane = lax.broadcasted_iota(jnp.int32, (TQ, LANES), 1)
    q = q_ref[...] * (1.0 / math.sqrt(HEAD_DIM))
    zero = jnp.zeros_like(q)
    qh = [jnp.where(lane < HEAD_DIM, q, zero), jnp.where(lane >= HEAD_DIM, q, zero)]
    later = later_ref[...]
    row = lax.broadcasted_iota(jnp.int32, (TK, TK), 0)
    col = lax.broadcasted_iota(jnp.int32, (TK, TK), 1)
    causal = col < row

    def sweep(h, rows, kb, masked, carry):
        start = pl.multiple_of(kb * TK, TK)
        z = _dot_nt(qh[h][rows], k_ref[pl.ds(start, TK), :])
        sp = jnp.maximum(z, 0.0) + jnp.log(1.0 + jnp.exp(-jnp.abs(z)))
        spm = jnp.where(causal, sp, 0.0) if masked else sp
        rest = _dot(spm.astype(BF16), later) + carry
        a = jnp.exp(z - sp - rest)
        if masked:
            a = jnp.where(causal, a, 0.0)
        contrib = _dot(a.astype(BF16), v_ref[pl.ds(start, TK), :])
        return contrib, carry + jnp.sum(spm, axis=1, keepdims=True)

    top = slice(0, TK)
    bot = slice(TK, TQ)
    no_mass = jnp.zeros((TK, 1), F32)
    has_prev = qi > 0
    for h in range(2):
        acc_t, c_t = sweep(h, top, 2 * qi, True, no_mass)
        acc_t2, c_t2 = sweep(h, top, jnp.maximum(2 * qi - 1, 0), False, c_t)
        acc_b, c_b = sweep(h, bot, 2 * qi + 1, True, no_mass)
        acc_b2, c_b = sweep(h, bot, 2 * qi, False, c_b)
        acc_ref[h, top, :] = acc_t + jnp.where(has_prev, acc_t2, 0.0)
        acc_ref[h, bot, :] = acc_b + acc_b2
        carry_ref[h, top, :] = jnp.where(has_prev, c_t2, c_t)
        carry_ref[h, bot, :] = c_b

    def min_mass():
        return jnp.minimum(jnp.min(carry_ref[0]), jnp.min(carry_ref[1]))

    def more(state):
        j, mass = state
        return jnp.logical_and(j < 2 * qi, mass < SB_ZERO_MASS)

    def earlier(state):
        j, _ = state
        kb_bot = 2 * qi - 1 - j
        kb_top = kb_bot - 1
        top_live = kb_top >= 0
        for h in range(2):
            contrib, c = sweep(h, top, jnp.maximum(kb_top, 0), False, carry_ref[h, top, :])
            acc_ref[h, top, :] += jnp.where(top_live, contrib, 0.0)
            carry_ref[h, top, :] = jnp.where(top_live, c, carry_ref[h, top, :])
            contrib, c = sweep(h, bot, kb_bot, False, carry_ref[h, bot, :])
            acc_ref[h, bot, :] += contrib
            carry_ref[h, bot, :] = c
        return j + 1, min_mass()

    lax.while_loop(more, earlier, (jnp.int32(0), min_mass()))
    o_ref[...] = jnp.where(lane < HEAD_DIM, acc_ref[0], acc_ref[1]).astype(o_ref.dtype)


def _sb_attention(h3, kv3):
    B, S, _ = h3.shape
    TK = SB_TK
    TQ = 2 * TK
    pairs = SEQ_MIX_W // LANES
    later = jnp.asarray(np.arange(TK)[:, None] > np.arange(TK)[None, :], dtype=BF16)
    return pl.pallas_call(
        _sb_kernel,
        grid=(B, pairs, S // TQ),
        in_specs=[pl.BlockSpec((None, TQ, LANES), lambda b, p, i: (b, i, p)),
                  pl.BlockSpec((None, S, LANES), lambda b, p, i: (b, 0, p)),
                  pl.BlockSpec((None, S, LANES), lambda b, p, i: (b, 0, pairs + p)),
                  pl.BlockSpec((TK, TK), lambda b, p, i: (0, 0))],
        out_specs=pl.BlockSpec((None, TQ, LANES), lambda b, p, i: (b, i, p)),
        out_shape=jax.ShapeDtypeStruct((B, S, SEQ_MIX_W), BF16),
        scratch_shapes=[pltpu.VMEM((2, TQ, LANES), F32), pltpu.VMEM((2, TQ, 1), F32)],
        compiler_params=_cparams(("parallel", "parallel", "arbitrary"), 32),
        name="stick_breaking",
    )(h3, kv3, kv3, later)


def _mem_kernel(q_ref, kv_ref, o_ref):
    q = q_ref[...] * (1.0 / math.sqrt(HEAD_DIM))
    outs = []
    for h in range(MEM_HEADS):
        lanes = slice(h * HEAD_DIM, (h + 1) * HEAD_DIM)
        vl = slice(MEM_W + h * HEAD_DIM, MEM_W + (h + 1) * HEAD_DIM)
        s = _dot_nt(q[:, lanes].astype(BF16), kv_ref[:, lanes].astype(BF16))
        e = jnp.exp(s - jnp.max(s, axis=-1, keepdims=True))
        den = jnp.sum(e, axis=-1, keepdims=True)
        outs.append(_dot(e.astype(BF16), kv_ref[:, vl].astype(BF16)) / den)
    o_ref[...] = jnp.concatenate(outs, axis=1).astype(o_ref.dtype)


def _mem_attention(h3, mem_kv3, q_col_block, tm):
    B, S, _ = h3.shape
    return pl.pallas_call(
        _mem_kernel,
        grid=(B, S // tm),
        in_specs=[pl.BlockSpec((None, tm, MEM_W), lambda b, i: (b, i, q_col_block)),
                  pl.BlockSpec((None, MEM_LEN, 2 * MEM_W), lambda b, i: (b, 0, 0))],
        out_specs=pl.BlockSpec((None, tm, MEM_W), lambda b, i: (b, i, 0)),
        out_shape=jax.ShapeDtypeStruct((B, S, MEM_W), BF16),
        compiler_params=_cparams(("parallel", "parallel"), 32),
        name="mem_attention",
    )(h3, mem_kv3)


def _route(lg):
    lane = lax.broadcasted_iota(jnp.int32, lg.shape, 1)
    lane_f = lane.astype(F32)
    neg = -jnp.inf
    far = float(LANES)

    lgg = jnp.where((lane >= N_EXPERTS) & (lane < N_EXPERTS + N_GROUPS), lg, neg)
    gmax = jnp.max(lgg, axis=-1, keepdims=True)
    p_top = 1.0 / jnp.sum(jnp.exp(lgg - gmax), axis=-1, keepdims=True)
    g_lane = jnp.min(jnp.where(lgg == gmax, lane_f, far), axis=-1, keepdims=True)
    first = (g_lane - float(N_EXPERTS)) * float(EXPERTS_PER_GROUP)

    le = jnp.where((lane_f >= first) & (lane_f < first + float(EXPERTS_PER_GROUP)), lg, neg)
    m1 = jnp.max(le, axis=-1, keepdims=True)
    i1 = jnp.min(jnp.where(le == m1, lane_f, far), axis=-1, keepdims=True)
    le2 = jnp.where(lane_f == i1, neg, le)
    m2 = jnp.max(le2, axis=-1, keepdims=True)
    i2 = jnp.min(jnp.where(le2 == m2, lane_f, far), axis=-1, keepdims=True)
    e2 = jnp.exp(m2 - m1)
    den = 1.0 + e2
    gates = p_top * (jnp.where(lane_f == i1, 1.0 / den, 0.0) + jnp.where(lane_f == i2, e2 / den, 0.0))
    return gates, g_lane - float(N_EXPERTS)


def _oproj_kernel(seq_ref, mem_ref, x_ref, wo_ref, g_ref, b_ref, wr_ref, br_ref, before_ref,
                  o_ref, gates_ref, drow_ref, cnt_ref, wob_ref, wrh_ref, wrl_ref):
    @pl.when(pl.program_id(0) == 0)
    def _():
        wob_ref[...] = wo_ref[...].astype(BF16)
        hi, lo = _split_bf16(wr_ref[...])
        wrh_ref[...] = hi
        wrl_ref[...] = lo

    for t in range(OPROJ_TILES):
        rows = slice(t * MOE_TM, (t + 1) * MOE_TM)
        meta = slice(8 * t, 8 * (t + 1))
        y = (_dot(seq_ref[rows, :], wob_ref[0:SEQ_MIX_W, :])
             + _dot(mem_ref[rows, :], wob_ref[SEQ_MIX_W:, :]))
        x1 = _layer_norm(DEEPNORM_ALPHA * x_ref[rows, :] + y, g_ref[...], b_ref[...])
        o_ref[rows, :] = x1
        xh, xl = _split_bf16(x1)
        wrh = wrh_ref[...]
        lg2 = _dot(xh, jnp.concatenate([wrh, wrl_ref[...]], axis=1))
        gates, grp = _route(lg2[:, :LANES] + lg2[:, LANES:] + _dot(xl, wrh) + br_ref[...])

        lane_f = lax.broadcasted_iota(jnp.int32, gates.shape, 1).astype(F32)
        onehot = jnp.where(lane_f == grp, 1.0, 0.0)
        ranks = _dot(before_ref[...], onehot.astype(BF16))
        rank = jnp.sum(jnp.where(lane_f == grp, ranks, 0.0), axis=-1, keepdims=True)
        counts = jnp.sum(onehot, axis=0, keepdims=True)
        padded = jnp.floor((counts + float(MOE_ALIGN - 1)) * (1.0 / MOE_ALIGN)) * float(MOE_ALIGN)
        offset = jnp.sum(jnp.where(lane_f < grp, padded, 0.0), axis=-1, keepdims=True)
        dest = offset + rank
        gates_ref[rows, :] = gates + jnp.where(lane_f == float(MOE_DEST_LANE), dest, 0.0)
        cnt_ref[meta, :] = jnp.broadcast_to(counts, (8, LANES))
        hi16 = jnp.floor(dest * (1.0 / 16.0))
        digits = (jnp.where(lane_f == 0.0, hi16, 0.0)
                  + jnp.where(lane_f == 1.0, dest - 16.0 * hi16, 0.0))
        sel_lane = lax.broadcasted_iota(jnp.int32, (8, LANES), 1)
        sel = jnp.where(sel_lane == 0, 16.0, jnp.where(sel_lane == 1, 1.0, 0.0)).astype(BF16)
        drow_ref[meta, :] = _dot_nt(sel, digits.astype(BF16))


def _oproj_ln_route(seq, memo, x, w_o, layer, g, b, w_group, b_group, w_router, b_router):
    T = x.shape[0]
    tm = MOE_TM
    step_rows = OPROJ_TILES * tm
    pad = LANES - N_EXPERTS - N_GROUPS
    wr = jnp.concatenate([w_router, w_group, jnp.zeros((D_MODEL, pad), F32)], axis=1)
    br = jnp.concatenate([b_router, b_group, jnp.zeros((pad,), F32)]).reshape(1, LANES)
    before = jnp.asarray(np.arange(tm)[None, :] < np.arange(tm)[:, None], dtype=BF16)
    const = lambda i: (0, 0)
    return pl.pallas_call(
        _oproj_kernel,
        grid=(T // step_rows,),
        in_specs=[pl.BlockSpec((step_rows, SEQ_MIX_W), lambda i: (i, 0)),
                  pl.BlockSpec((step_rows, MEM_W), lambda i: (i, 0)),
                  pl.BlockSpec((step_rows, D_MODEL), lambda i: (i, 0)),
                  pl.BlockSpec((None, D_MODEL, D_MODEL), lambda i: (layer, 0, 0)),
                  pl.BlockSpec((1, D_MODEL), const),
                  pl.BlockSpec((1, D_MODEL), const),
                  pl.BlockSpec((D_MODEL, LANES), const),
                  pl.BlockSpec((1, LANES), const),
                  pl.BlockSpec((tm, tm), const)],
        out_specs=[pl.BlockSpec((step_rows, D_MODEL), lambda i: (i, 0)),
                   pl.BlockSpec((step_rows, LANES), lambda i: (i, 0)),
                   pl.BlockSpec((8 * OPROJ_TILES, tm), lambda i: (i, 0)),
                   pl.BlockSpec((8 * OPROJ_TILES, LANES), lambda i: (i, 0))],
        out_shape=[jax.ShapeDtypeStruct((T, D_MODEL), F32),
                   jax.ShapeDtypeStruct((T, LANES), F32),
                   jax.ShapeDtypeStruct((8 * (T // tm), tm), F32),
                   jax.ShapeDtypeStruct((8 * (T // tm), LANES), F32)],
        scratch_shapes=[pltpu.VMEM((D_MODEL, D_MODEL), BF16),
                        pltpu.VMEM((D_MODEL, LANES), BF16),
                        pltpu.VMEM((D_MODEL, LANES), BF16)],
        compiler_params=_cparams(("arbitrary",), 48),
        name="oproj_ln_route",
    )(seq, memo, x, w_o, g.reshape(1, D_MODEL), b.reshape(1, D_MODEL), wr, br, before)


def _moe_kernel(npairs_ref, pstart_ref, pgrp_ref,
                x_ref, gates_ref, drow_ref, wg_ref, wu_ref, wd_ref, g_ref, b_ref, o_ref,
                wgu_s, wd_s, xs_ref, gs_ref, ys_ref):
    step = pl.program_id(0)
    tm = MOE_TM
    R = MOE_R
    nrows = MOE_SORT_ROWS

    @pl.when(step < N_EXPERTS)
    def _():
        wgu_s[step, :, 0:D_EXPERT] = wg_ref[...].astype(BF16)
        wgu_s[step, :, D_EXPERT:] = wu_ref[...].astype(BF16)
        row0 = pl.multiple_of((step % EXPERTS_PER_GROUP) * D_EXPERT, D_EXPERT)
        wd_s[step // EXPERTS_PER_GROUP, pl.ds(row0, D_EXPERT), :] = wd_ref[...].astype(BF16)

    @pl.when(step >= N_EXPERTS)
    def _():
        tile = step - N_EXPERTS
        x = x_ref[...]
        gates = gates_ref[...]
        lane_g = lax.broadcasted_iota(jnp.int32, gates.shape, 1)
        dest_col = jnp.sum(jnp.where(lane_g == MOE_DEST_LANE, gates, 0.0), axis=-1, keepdims=True)
        used = MOE_TM + N_GROUPS * MOE_ALIGN
        sorted_row = lax.broadcasted_iota(jnp.int32, (used, tm), 0).astype(F32)
        perm = jnp.where(sorted_row == drow_ref[0:1, :], 1.0, 0.0).astype(BF16)
        xs_ref[0:used, :] = _dot(perm, x.astype(BF16)).astype(BF16)
        xs_ref[used:, :] = jnp.zeros((nrows - used, D_MODEL), BF16)
        ghi, glo = _split_bf16(gates)
        g2 = _dot(perm, jnp.concatenate([ghi, glo], axis=1))
        gs_ref[0:used, :] = g2[:, :LANES] + g2[:, LANES:]
        gs_ref[used:, :] = jnp.zeros((nrows - used, LANES), F32)
        ys_ref[...] = jnp.zeros_like(ys_ref)

        def pair(p, _):
            start = pstart_ref[tile * MOE_MAX_PAIRS + p]
            grp = pgrp_ref[tile * MOE_MAX_PAIRS + p]
            rows = pl.ds(pl.multiple_of(start, MOE_ALIGN), R)
            xblk = xs_ref[rows, :]
            gblk = gs_ref[rows, :]
            lane = lax.broadcasted_iota(jnp.int32, gblk.shape, 1)
            hs = []
            for j in range(EXPERTS_PER_GROUP):
                e = grp * EXPERTS_PER_GROUP + j
                hgu = _dot(xblk, wgu_s[e])
                hg = hgu[:, :D_EXPERT]
                ge = jnp.sum(jnp.where(lane == e, gblk, 0.0), axis=-1, keepdims=True)
                hs.append((hg * jax.nn.sigmoid(hg) * hgu[:, D_EXPERT:] * ge).astype(BF16))
            ys_ref[rows, :] += _dot(jnp.concatenate(hs, axis=1), wd_s[grp])
            return 0

        lax.fori_loop(0, npairs_ref[tile], pair, 0)

        sorted_col = lax.broadcasted_iota(jnp.int32, (tm, nrows), 1).astype(F32)
        unperm = jnp.where(sorted_col == dest_col, 1.0, 0.0).astype(BF16)
        y = _dot(unperm, ys_ref[...].astype(BF16))
        o_ref[...] = _layer_norm(DEEPNORM_ALPHA * x + y, g_ref[...], b_ref[...])


def _moe_plan(counts):
    ntiles = counts.shape[0]
    max_blocks = -(-MOE_TM // MOE_R)
    padded = (counts + (MOE_ALIGN - 1)) // MOE_ALIGN * MOE_ALIGN
    seg_start = jnp.cumsum(padded, axis=1) - padded
    nblk = (counts + (MOE_R - 1)) // MOE_R
    j = jnp.arange(max_blocks, dtype=jnp.int32)[None, None, :]
    active = (j < nblk[:, :, None]).reshape(ntiles, N_GROUPS * max_blocks)
    order = jnp.argsort(jnp.logical_not(active), axis=1, stable=True)[:, :MOE_MAX_PAIRS].astype(jnp.int32)
    grp = order // max_blocks
    start = jnp.take_along_axis(seg_start, grp, axis=1) + (order % max_blocks) * MOE_R
    npairs = jnp.sum(active, axis=1).astype(jnp.int32)
    return npairs, start.astype(jnp.int32).reshape(-1), grp.reshape(-1)


def _moe_ln(x, gates, dest_rows, counts8, w_gate, w_up, w_down, layer, g, b):
    T = x.shape[0]
    tm = MOE_TM
    ntiles = T // tm
    counts = counts8.reshape(ntiles, 8, LANES)[:, 0, :N_GROUPS].astype(jnp.int32)
    npairs, pstart, pgrp = _moe_plan(counts)

    def tile_map(s, *_):
        return (jnp.maximum(s - N_EXPERTS, 0), 0)

    def w_map(s, *_):
        return (layer, jnp.minimum(s, N_EXPERTS - 1), 0, 0)

    const = lambda s, *_: (0, 0)
    grid_spec = pltpu.PrefetchScalarGridSpec(
        num_scalar_prefetch=3,
        grid=(N_EXPERTS + ntiles,),
        in_specs=[pl.BlockSpec((tm, D_MODEL), tile_map),
                  pl.BlockSpec((tm, LANES), tile_map),
                  pl.BlockSpec((8, tm), tile_map),
                  pl.BlockSpec((None, None, D_MODEL, D_EXPERT), w_map),
                  pl.BlockSpec((None, None, D_MODEL, D_EXPERT), w_map),
                  pl.BlockSpec((None, None, D_EXPERT, D_MODEL), w_map),
                  pl.BlockSpec((1, D_MODEL), const),
                  pl.BlockSpec((1, D_MODEL), const)],
        out_specs=pl.BlockSpec((tm, D_MODEL), tile_map),
        scratch_shapes=[pltpu.VMEM((N_EXPERTS, D_MODEL, 2 * D_EXPERT), BF16),
                        pltpu.VMEM((N_GROUPS, EXPERTS_PER_GROUP * D_EXPERT, D_MODEL), BF16),
                        pltpu.VMEM((MOE_SORT_ROWS, D_MODEL), BF16),
                        pltpu.VMEM((MOE_SORT_ROWS, LANES), F32),
                        pltpu.VMEM((MOE_SORT_ROWS, D_MODEL), F32)])
    return pl.pallas_call(
        _moe_kernel,
        grid_spec=grid_spec,
        out_shape=jax.ShapeDtypeStruct((T, D_MODEL), F32),
        compiler_params=_cparams(("arbitrary",), 56),
        name="moe_ln",
    )(npairs, pstart, pgrp, x, gates, dest_rows, w_gate, w_up, w_down,
      g.reshape(1, D_MODEL), b.reshape(1, D_MODEL))


def kernel(x, mem, a_w_in, a_lower_bounds, a_gnorm, b_w_in, w_kv_shared, w_mem_kv, w_o,
           ln_mix_g, ln_mix_b, ln_ffn_g, ln_ffn_b, w_group, b_group, w_router, b_router,
           w_gate, w_up, w_down):
    B, S, D = x.shape
    T = B * S
    xf = x.reshape(T, D)
    memf = mem.reshape(B * MEM_LEN, D)
    a_cols = a_w_in.shape[-1]
    b_cols = b_w_in.shape[-1]
    kv3 = None
    for layer in range(DEPTH):
        mem_kv = _matmul(memf, w_mem_kv, layer, 512, 2 * MEM_W, "mem_kv")
        mem_kv3 = mem_kv.reshape(B, MEM_LEN, 2 * MEM_W)
        if layer < N_A_LAYERS:
            h, f_raw = _in_proj_a(xf, a_w_in, layer, 256)
            h3 = h.reshape(B, S, a_cols)
            seq = _hgrn(h3, f_raw.reshape(B, S, SEQ_MIX_W), a_lower_bounds, a_gnorm[layer], layer)
            q_col_block = 4 * SEQ_MIX_W // MEM_W
        else:
            if layer == N_A_LAYERS:
                kv = _matmul(xf, w_kv_shared[None], 0, 512, 2 * SEQ_MIX_W, "kv_shared", BF16)
                kv3 = kv.reshape(B, S, 2 * SEQ_MIX_W)
            h = _matmul(xf, b_w_in, layer - N_A_LAYERS, 512, b_cols, "in_proj_b", BF16)
            h3 = h.reshape(B, S, b_cols)
            seq = _sb_attention(h3, kv3)
            q_col_block = SEQ_MIX_W // MEM_W
        memo = _mem_attention(h3, mem_kv3, q_col_block, 2048)
        x1, gates, dest_rows, counts8 = _oproj_ln_route(
            seq.reshape(T, SEQ_MIX_W), memo.reshape(T, MEM_W), xf, w_o, layer,
            ln_mix_g[layer], ln_mix_b[layer],
            w_group[layer], b_group[layer], w_router[layer], b_router[layer])
        xf = _moe_ln(x1, gates, dest_rows, counts8, w_gate, w_up, w_down, layer,
                     ln_ffn_g[layer], ln_ffn_b[layer])
    return xf.reshape(B, S, D)
```

```python
import functools
import math

import numpy as np
import jax
import jax.numpy as jnp
from jax import lax
from jax.experimental import pallas as pl
from jax.experimental.pallas import tpu as pltpu

F32 = jnp.float32
BF16 = jnp.bfloat16

D_MODEL = 1024
DEPTH = 4
N_A_LAYERS = DEPTH // 2
MEM_LEN = 256
MEM_HEADS = 4
HEAD_DIM = 64
MEM_W = MEM_HEADS * HEAD_DIM
SEQ_MIX_W = D_MODEL - MEM_W
HGRN_DK = 128
HGRN_HEADS = SEQ_MIX_W // HGRN_DK
SB_HEADS = SEQ_MIX_W // HEAD_DIM
N_GROUPS = 4
EXPERTS_PER_GROUP = 4
N_EXPERTS = N_GROUPS * EXPERTS_PER_GROUP
D_EXPERT = 256
DEEPNORM_ALPHA = (2 * DEPTH) ** 0.25
LN_EPS = 1e-5
RMS_EPS = 1e-6

LANES = 128
HGRN_C = 256
SB_TK = 256
SB_ZERO_MASS = 110.0
MOE_TM = 512
MOE_R = 144
MOE_ALIGN = 16
MOE_MAX_PAIRS = MOE_TM // MOE_R + N_GROUPS
MOE_SORT_ROWS = -(-(MOE_TM + N_GROUPS * MOE_ALIGN + MOE_R) // 256) * 256
MOE_DEST_LANE = N_EXPERTS
OPROJ_TILES = 1
MIB = 1024 * 1024


def _cparams(semantics, vmem_mib):
    return pltpu.CompilerParams(dimension_semantics=semantics, vmem_limit_bytes=vmem_mib * MIB)


def _dot(a, b):
    return jnp.dot(a, b, preferred_element_type=F32)


def _dot_nt(a, b):
    return lax.dot_general(a, b, (((1,), (1,)), ((), ())), preferred_element_type=F32)


def _dot_tn(a, b):
    return lax.dot_general(a, b, (((0,), (0,)), ((), ())), preferred_element_type=F32)


def _split_bf16(x):
    hi = x.astype(BF16)
    lo = (x - hi.astype(F32)).astype(BF16)
    return hi, lo


def _layer_norm(r, g, b):
    mu = jnp.mean(r, axis=-1, keepdims=True)
    d = r - mu
    var = jnp.mean(d * d, axis=-1, keepdims=True)
    return d * lax.rsqrt(var + LN_EPS) * g + b


def _mm_kernel(x_ref, w_ref, o_ref, wb_ref):
    @pl.when(pl.program_id(1) == 0)
    def _():
        wb_ref[...] = w_ref[...].astype(BF16)

    o_ref[...] = _dot(x_ref[...].astype(BF16), wb_ref[...]).astype(o_ref.dtype)


def _matmul(x, w_stack, layer, tm, tn, name, out_dtype=F32):
    M, K = x.shape
    N = w_stack.shape[2]
    assert M % tm == 0 and N % tn == 0
    return pl.pallas_call(
        _mm_kernel,
        grid=(N // tn, M // tm),
        in_specs=[pl.BlockSpec((tm, K), lambda j, i: (i, 0)),
                  pl.BlockSpec((None, K, tn), lambda j, i: (layer, 0, j))],
        out_specs=pl.BlockSpec((tm, tn), lambda j, i: (i, j)),
        out_shape=jax.ShapeDtypeStruct((M, N), out_dtype),
        scratch_shapes=[pltpu.VMEM((K, tn), BF16)],
        compiler_params=_cparams(("parallel", "arbitrary"), 48),
        name=name,
    )(x, w_stack)


def _in_proj_a_kernel(x_ref, w_ref, o_ref, f_ref, wb_ref):
    @pl.when(pl.program_id(0) == 0)
    def _():
        wb_ref[...] = w_ref[...].astype(BF16)

    acc = _dot(x_ref[...].astype(BF16), wb_ref[...])
    o_ref[...] = acc.astype(BF16)
    f_ref[...] = acc[:, SEQ_MIX_W:2 * SEQ_MIX_W]


def _in_proj_a(x, w_stack, layer, tm):
    M, K = x.shape
    N = w_stack.shape[2]
    return pl.pallas_call(
        _in_proj_a_kernel,
        grid=(M // tm,),
        in_specs=[pl.BlockSpec((tm, K), lambda i: (i, 0)),
                  pl.BlockSpec((None, K, N), lambda i: (layer, 0, 0))],
        out_specs=[pl.BlockSpec((tm, N), lambda i: (i, 0)),
                   pl.BlockSpec((tm, SEQ_MIX_W), lambda i: (i, 0))],
        out_shape=[jax.ShapeDtypeStruct((M, N), BF16),
                   jax.ShapeDtypeStruct((M, SEQ_MIX_W), F32)],
        scratch_shapes=[pltpu.VMEM((K, N), BF16)],
        compiler_params=_cparams(("arbitrary",), 52),
        name="in_proj_a",
    )(x, w_stack)


def _hgrn_levels():
    m = HGRN_C // 2
    out = []
    while m >= 1:
        out.append(m)
        m //= 2
    return out


def _hgrn_level_decay(m, G, f, row):
    C = HGRN_C
    if m >= 4:
        G3 = G.reshape(C // (2 * m), 2 * m, LANES)
        return jnp.exp(-jnp.abs(G3 - G3[:, m - 1:m, :])).reshape(C, LANES)
    if m == 2:
        f_next = pltpu.roll(f, C - 1, axis=0)
        f_prev = pltpu.roll(f, 1, axis=0)
        p = row & 3
        return jnp.where(p == 0, f_next, jnp.where(p == 1, 1.0, jnp.where(p == 2, f, f * f_prev)))
    return jnp.where((row & 1) == 1, f, 1.0)


def _hgrn_kernel(layer, q_ref, f_ref, i_ref, g_ref, lbp_ref, gn_ref, incl_ref, o_ref, st_ref):
    C = HGRN_C

    @pl.when(pl.program_id(1) == 0)
    def _():
        st_ref[...] = jnp.zeros_like(st_ref)

    a = lbp_ref[...]
    ea = jnp.exp(a - jnp.max(a, axis=0, keepdims=True))
    pa = ea / jnp.sum(ea, axis=0, keepdims=True)
    lb_all = jnp.zeros((1, SEQ_MIX_W), F32)
    for l in range(1, layer + 1):
        lb_all = lb_all + pa[l:l + 1]

    t_idx = lax.broadcasted_iota(jnp.int32, (C, C), 0)
    s_idx = lax.broadcasted_iota(jnp.int32, (C, C), 1)
    split = t_idx ^ s_idx
    levels = _hgrn_levels()
    incl = incl_ref[...]
    row = lax.broadcasted_iota(jnp.int32, (C, LANES), 0)

    for h in range(HGRN_HEADS):
        lanes = slice(h * LANES, (h + 1) * LANES)
        lb = lb_all[:, lanes]
        fr = f_ref[:, lanes]
        logsig = jnp.minimum(fr, 0.0) - jnp.log(1.0 + jnp.exp(-jnp.abs(fr)))
        la = jnp.log(lb)
        lc = jnp.log1p(-lb) + logsig
        logf = jnp.maximum(la, lc) + jnp.log(1.0 + jnp.exp(-jnp.abs(la - lc)))
        f = jnp.exp(logf)
        k = 1.0 - f

        hi = logf.astype(BF16)
        r1 = logf - hi.astype(F32)
        mid = r1.astype(BF16)
        lo = (r1 - mid.astype(F32)).astype(BF16)
        g3 = _dot(incl, jnp.concatenate([hi, mid, lo], axis=1))
        G = g3[:, :LANES] + g3[:, LANES:2 * LANES] + g3[:, 2 * LANES:]
        dec_g = jnp.exp(G)
        dec_suf = jnp.exp(G[C - 1:C, :] - G)

        qr = q_ref[:, lanes].astype(F32)
        q = qr * jax.nn.sigmoid(qr) * (HGRN_DK ** -0.5)
        v = i_ref[:, lanes].astype(BF16)

        sc = None
        for li in range(len(levels) - 1, -1, -1):
            m = levels[li]
            u = (jnp.where((row & m) != 0, q, k) * _hgrn_level_decay(m, G, f, row)).astype(BF16)
            p = _dot_nt(u, u)
            sc = p if sc is None else jnp.where(split >= m, p, sc)
        sc = jnp.where(s_idx < t_idx, sc, 0.0)
        sc = jnp.where(s_idx == t_idx, jnp.sum(q * k, axis=-1, keepdims=True), sc)

        st = st_ref[h]
        o = _dot(sc.astype(BF16), v) + _dot_nt((q * dec_g).astype(BF16), st.astype(BF16))
        st_ref[h] = st * dec_g[C - 1:C, :] + _dot_tn(v, (k * dec_suf).astype(BF16))

        o = o * lax.rsqrt(jnp.mean(o * o, axis=-1, keepdims=True) + RMS_EPS)
        gr = g_ref[:, lanes].astype(F32)
        o_ref[:, lanes] = (o * gn_ref[:, lanes] * (gr * jax.nn.sigmoid(gr))).astype(o_ref.dtype)


def _hgrn(h3, f3, a_lower_bounds, gnorm, layer):
    B, S, _ = h3.shape
    C = HGRN_C
    W = SEQ_MIX_W
    incl = jnp.asarray(np.arange(C)[None, :] <= np.arange(C)[:, None], dtype=BF16)

    def col(j):
        return pl.BlockSpec((None, C, W), lambda b, c: (b, c, j))

    return pl.pallas_call(
        functools.partial(_hgrn_kernel, layer),
        grid=(B, S // C),
        in_specs=[col(0), col(0), col(2), col(3),
                  pl.BlockSpec((N_A_LAYERS, W), lambda b, c: (0, 0)),
                  pl.BlockSpec((1, W), lambda b, c: (0, 0)),
                  pl.BlockSpec((C, C), lambda b, c: (0, 0))],
        out_specs=pl.BlockSpec((None, C, W), lambda b, c: (b, c, 0)),
        out_shape=jax.ShapeDtypeStruct((B, S, W), BF16),
        scratch_shapes=[pltpu.VMEM((HGRN_HEADS, LANES, HGRN_DK), F32)],
        compiler_params=_cparams(("parallel", "arbitrary"), 32),
        name="hgrn2",
    )(h3, f3, h3, h3, a_lower_bounds, gnorm.reshape(1, W), incl)


def _sb_kernel(q_ref, k_ref, v_ref, later_ref, o_ref, acc_ref, carry_ref):
    TK = SB_TK
    TQ = 2 * TK
    qi = pl.program_id(2)
    lane = lax.broadcasted_iota(jnp.int32, (TQ, LANES), 1)
    q = q_ref[...] * (1.0 / math.sqrt(HEAD_DIM))
    zero = jnp.zeros_like(q)
    qh = [jnp.where(lane < HEAD_DIM, q, zero), jnp.where(lane >= HEAD_DIM, q, zero)]
    later = later_ref[...]
    row = lax.broadcasted_iota(jnp.int32, (TK, TK), 0)
    col = lax.broadcasted_iota(jnp.int32, (TK, TK), 1)
    causal = col < row

    def sweep(h, rows, kb, masked, carry):
        start = pl.multiple_of(kb * TK, TK)
        z = _dot_nt(qh[h][rows], k_ref[pl.ds(start, TK), :])
        sp = jnp.maximum(z, 0.0) + jnp.log(1.0 + jnp.exp(-jnp.abs(z)))
        spm = jnp.where(causal, sp, 0.0) if masked else sp
        rest = _dot(spm.astype(BF16), later) + carry
        a = jnp.exp(z - sp - rest)
        if masked:
            a = jnp.where(causal, a, 0.0)
        contrib = _dot(a.astype(BF16), v_ref[pl.ds(start, TK), :])
        return contrib, carry + jnp.sum(spm, axis=1, keepdims=True)

    top = slice(0, TK)
    bot = slice(TK, TQ)
    no_mass = jnp.zeros((TK, 1), F32)
    has_prev = qi > 0
    for h in range(2):
        acc_t, c_t = sweep(h, top, 2 * qi, True, no_mass)
        acc_t2, c_t2 = sweep(h, top, jnp.maximum(2 * qi - 1, 0), False, c_t)
        acc_b, c_b = sweep(h, bot, 2 * qi + 1, True, no_mass)
        acc_b2, c_b = sweep(h, bot, 2 * qi, False, c_b)
        acc_ref[h, top, :] = acc_t + jnp.where(has_prev, acc_t2, 0.0)
        acc_ref[h, bot, :] = acc_b + acc_b2
        carry_ref[h, top, :] = jnp.where(has_prev, c_t2, c_t)
        carry_ref[h, bot, :] = c_b

    def min_mass():
        return jnp.minimum(jnp.min(carry_ref[0]), jnp.min(carry_ref[1]))

    def more(state):
        j, mass = state
        return jnp.logical_and(j < 2 * qi, mass < SB_ZERO_MASS)

    def earlier(state):
        j, _ = state
        kb_bot = 2 * qi - 1 - j
        kb_top = kb_bot - 1
        top_live = kb_top >= 0
        for h in range(2):
            contrib, c = sweep(h, top, jnp.maximum(kb_top, 0), False, carry_ref[h, top, :])
            acc_ref[h, top, :] += jnp.where(top_live, contrib, 0.0)
            carry_ref[h, top, :] = jnp.where(top_live, c, carry_ref[h, top, :])
            contrib, c = sweep(h, bot, kb_bot, False, carry_ref[h, bot, :])
            acc_ref[h, bot, :] += contrib
            carry_ref[h, bot, :] = c
        return j + 1, min_mass()

    lax.while_loop(more, earlier, (jnp.int32(0), min_mass()))
    o_ref[...] = jnp.where(lane < HEAD_DIM, acc_ref[0], acc_ref[1]).astype(o_ref.dtype)


def _sb_attention(h3, kv3):
    B, S, _ = h3.shape
    TK = SB_TK
    TQ = 2 * TK
    pairs = SEQ_MIX_W // LANES
    later = jnp.asarray(np.arange(TK)[:, None] > np.arange(TK)[None, :], dtype=BF16)
    return pl.pallas_call(
        _sb_kernel,
        grid=(B, pairs, S // TQ),
        in_specs=[pl.BlockSpec((None, TQ, LANES), lambda b, p, i: (b, i, p)),
                  pl.BlockSpec((None, S, LANES), lambda b, p, i: (b, 0, p)),
                  pl.BlockSpec((None, S, LANES), lambda b, p, i: (b, 0, pairs + p)),
                  pl.BlockSpec((TK, TK), lambda b, p, i: (0, 0))],
        out_specs=pl.BlockSpec((None, TQ, LANES), lambda b, p, i: (b, i, p)),
        out_shape=jax.ShapeDtypeStruct((B, S, SEQ_MIX_W), BF16),
        scratch_shapes=[pltpu.VMEM((2, TQ, LANES), F32), pltpu.VMEM((2, TQ, 1), F32)],
        compiler_params=_cparams(("parallel", "parallel", "arbitrary"), 32),
        name="stick_breaking",
    )(h3, kv3, kv3, later)


def _mem_kernel(q_ref, kv_ref, o_ref):
    q = q_ref[...] * (1.0 / math.sqrt(HEAD_DIM))
    outs = []
    for h in range(MEM_HEADS):
        lanes = slice(h * HEAD_DIM, (h + 1) * HEAD_DIM)
        vl = slice(MEM_W + h * HEAD_DIM, MEM_W + (h + 1) * HEAD_DIM)
        s = _dot_nt(q[:, lanes].astype(BF16), kv_ref[:, lanes].astype(BF16))
        e = jnp.exp(s - jnp.max(s, axis=-1, keepdims=True))
        den = jnp.sum(e, axis=-1, keepdims=True)
        outs.append(_dot(e.astype(BF16), kv_ref[:, vl].astype(BF16)) / den)
    o_ref[...] = jnp.concatenate(outs, axis=1).astype(o_ref.dtype)


def _mem_attention(h3, mem_kv3, q_col_block, tm):
    B, S, _ = h3.shape
    return pl.pallas_call(
        _mem_kernel,
        grid=(B, S // tm),
        in_specs=[pl.BlockSpec((None, tm, MEM_W), lambda b, i: (b, i, q_col_block)),
                  pl.BlockSpec((None, MEM_LEN, 2 * MEM_W), lambda b, i: (b, 0, 0))],
        out_specs=pl.BlockSpec((None, tm, MEM_W), lambda b, i: (b, i, 0)),
        out_shape=jax.ShapeDtypeStruct((B, S, MEM_W), BF16),
        compiler_params=_cparams(("parallel", "parallel"), 32),
        name="mem_attention",
    )(h3, mem_kv3)


def _route(lg):
    lane = lax.broadcasted_iota(jnp.int32, lg.shape, 1)
    lane_f = lane.astype(F32)
    neg = -jnp.inf
    far = float(LANES)

    lgg = jnp.where((lane >= N_EXPERTS) & (lane < N_EXPERTS + N_GROUPS), lg, neg)
    gmax = jnp.max(lgg, axis=-1, keepdims=True)
    p_top = 1.0 / jnp.sum(jnp.exp(lgg - gmax), axis=-1, keepdims=True)
    g_lane = jnp.min(jnp.where(lgg == gmax, lane_f, far), axis=-1, keepdims=True)
    first = (g_lane - float(N_EXPERTS)) * float(EXPERTS_PER_GROUP)

    le = jnp.where((lane_f >= first) & (lane_f < first + float(EXPERTS_PER_GROUP)), lg, neg)
    m1 = jnp.max(le, axis=-1, keepdims=True)
    i1 = jnp.min(jnp.where(le == m1, lane_f, far), axis=-1, keepdims=True)
    le2 = jnp.where(lane_f == i1, neg, le)
    m2 = jnp.max(le2, axis=-1, keepdims=True)
    i2 = jnp.min(jnp.where(le2 == m2, lane_f, far), axis=-1, keepdims=True)
    e2 = jnp.exp(m2 - m1)
    den = 1.0 + e2
    gates = p_top * (jnp.where(lane_f == i1, 1.0 / den, 0.0) + jnp.where(lane_f == i2, e2 / den, 0.0))
    return gates, g_lane - float(N_EXPERTS)


def _oproj_kernel(seq_ref, mem_ref, x_ref, wo_ref, g_ref, b_ref, wr_ref, br_ref, before_ref,
                  o_ref, gates_ref, drow_ref, cnt_ref, wob_ref, wrh_ref, wrl_ref):
    @pl.when(pl.program_id(0) == 0)
    def _():
        wob_ref[...] = wo_ref[...].astype(BF16)
        hi, lo = _split_bf16(wr_ref[...])
        wrh_ref[...] = hi
        wrl_ref[...] = lo

    for t in range(OPROJ_TILES):
        rows = slice(t * MOE_TM, (t + 1) * MOE_TM)
        meta = slice(8 * t, 8 * (t + 1))
        y = (_dot(seq_ref[rows, :], wob_ref[0:SEQ_MIX_W, :])
             + _dot(mem_ref[rows, :], wob_ref[SEQ_MIX_W:, :]))
        x1 = _layer_norm(DEEPNORM_ALPHA * x_ref[rows, :] + y, g_ref[...], b_ref[...])
        o_ref[rows, :] = x1
        xh, xl = _split_bf16(x1)
        wrh = wrh_ref[...]
        lg2 = _dot(xh, jnp.concatenate([wrh, wrl_ref[...]], axis=1))
        gates, grp = _route(lg2[:, :LANES] + lg2[:, LANES:] + _dot(xl, wrh) + br_ref[...])

        lane_f = lax.broadcasted_iota(jnp.int32, gates.shape, 1).astype(F32)
        onehot = jnp.where(lane_f == grp, 1.0, 0.0)
        ranks = _dot(before_ref[...], onehot.astype(BF16))
        rank = jnp.sum(jnp.where(lane_f == grp, ranks, 0.0), axis=-1, keepdims=True)
        counts = jnp.sum(onehot, axis=0, keepdims=True)
        padded = jnp.floor((counts + float(MOE_ALIGN - 1)) * (1.0 / MOE_ALIGN)) * float(MOE_ALIGN)
        offset = jnp.sum(jnp.where(lane_f < grp, padded, 0.0), axis=-1, keepdims=True)
        dest = offset + rank
        gates_ref[rows, :] = gates + jnp.where(lane_f == float(MOE_DEST_LANE), dest, 0.0)
        cnt_ref[meta, :] = jnp.broadcast_to(counts, (8, LANES))
        hi16 = jnp.floor(dest * (1.0 / 16.0))
        digits = (jnp.where(lane_f == 0.0, hi16, 0.0)
                  + jnp.where(lane_f == 1.0, dest - 16.0 * hi16, 0.0))
        sel_lane = lax.broadcasted_iota(jnp.int32, (8, LANES), 1)
        sel = jnp.where(sel_lane == 0, 16.0, jnp.where(sel_lane == 1, 1.0, 0.0)).astype(BF16)
        drow_ref[meta, :] = _dot_nt(sel, digits.astype(BF16))


def _oproj_ln_route(seq, memo, x, w_o, layer, g, b, w_group, b_group, w_router, b_router):
    T = x.shape[0]
    tm = MOE_TM
    step_rows = OPROJ_TILES * tm
    pad = LANES - N_EXPERTS - N_GROUPS
    wr = jnp.concatenate([w_router, w_group, jnp.zeros((D_MODEL, pad), F32)], axis=1)
    br = jnp.concatenate([b_router, b_group, jnp.zeros((pad,), F32)]).reshape(1, LANES)
    before = jnp.asarray(np.arange(tm)[None, :] < np.arange(tm)[:, None], dtype=BF16)
    const = lambda i: (0, 0)
    return pl.pallas_call(
        _oproj_kernel,
        grid=(T // step_rows,),
        in_specs=[pl.BlockSpec((step_rows, SEQ_MIX_W), lambda i: (i, 0)),
                  pl.BlockSpec((step_rows, MEM_W), lambda i: (i, 0)),
                  pl.BlockSpec((step_rows, D_MODEL), lambda i: (i, 0)),
                  pl.BlockSpec((None, D_MODEL, D_MODEL), lambda i: (layer, 0, 0)),
                  pl.BlockSpec((1, D_MODEL), const),
                  pl.BlockSpec((1, D_MODEL), const),
                  pl.BlockSpec((D_MODEL, LANES), const),
                  pl.BlockSpec((1, LANES), const),
                  pl.BlockSpec((tm, tm), const)],
        out_specs=[pl.BlockSpec((step_rows, D_MODEL), lambda i: (i, 0)),
                   pl.BlockSpec((step_rows, LANES), lambda i: (i, 0)),
                   pl.BlockSpec((8 * OPROJ_TILES, tm), lambda i: (i, 0)),
                   pl.BlockSpec((8 * OPROJ_TILES, LANES), lambda i: (i, 0))],
        out_shape=[jax.ShapeDtypeStruct((T, D_MODEL), F32),
                   jax.ShapeDtypeStruct((T, LANES), F32),
                   jax.ShapeDtypeStruct((8 * (T // tm), tm), F32),
                   jax.ShapeDtypeStruct((8 * (T // tm), LANES), F32)],
        scratch_shapes=[pltpu.VMEM((D_MODEL, D_MODEL), BF16),
                        pltpu.VMEM((D_MODEL, LANES), BF16),
                        pltpu.VMEM((D_MODEL, LANES), BF16)],
        compiler_params=_cparams(("arbitrary",), 48),
        name="oproj_ln_route",
    )(seq, memo, x, w_o, g.reshape(1, D_MODEL), b.reshape(1, D_MODEL), wr, br, before)


def _moe_kernel(npairs_ref, pstart_ref, pgrp_ref,
                x_ref, gates_ref, drow_ref, wg_ref, wu_ref, wd_ref, g_ref, b_ref, o_ref,
                wgu_s, wd_s, xs_ref, gs_ref, ys_ref):
    step = pl.program_id(0)
    tm = MOE_TM
    R = MOE_R
    nrows = MOE_SORT_ROWS

    @pl.when(step < N_EXPERTS)
    def _():
        wgu_s[step, :, 0:D_EXPERT] = wg_ref[...].astype(BF16)
        wgu_s[step, :, D_EXPERT:] = wu_ref[...].astype(BF16)
        row0 = pl.multiple_of((step % EXPERTS_PER_GROUP) * D_EXPERT, D_EXPERT)
        wd_s[step // EXPERTS_PER_GROUP, pl.ds(row0, D_EXPERT), :] = wd_ref[...].astype(BF16)

    @pl.when(step >= N_EXPERTS)
    def _():
        tile = step - N_EXPERTS
        x = x_ref[...]
        gates = gates_ref[...]
        lane_g = lax.broadcasted_iota(jnp.int32, gates.shape, 1)
        dest_col = jnp.sum(jnp.where(lane_g == MOE_DEST_LANE, gates, 0.0), axis=-1, keepdims=True)
        used = MOE_TM + N_GROUPS * MOE_ALIGN
        sorted_row = lax.broadcasted_iota(jnp.int32, (used, tm), 0).astype(F32)
        perm = jnp.where(sorted_row == drow_ref[0:1, :], 1.0, 0.0).astype(BF16)
        xs_ref[0:used, :] = _dot(perm, x.astype(BF16)).astype(BF16)
        xs_ref[used:, :] = jnp.zeros((nrows - used, D_MODEL), BF16)
        ghi, glo = _split_bf16(gates)
        g2 = _dot(perm, jnp.concatenate([ghi, glo], axis=1))
        gs_ref[0:used, :] = g2[:, :LANES] + g2[:, LANES:]
        gs_ref[used:, :] = jnp.zeros((nrows - used, LANES), F32)
        ys_ref[...] = jnp.zeros_like(ys_ref)

        def pair(p, _):
            start = pstart_ref[tile * MOE_MAX_PAIRS + p]
            grp = pgrp_ref[tile * MOE_MAX_PAIRS + p]
            rows = pl.ds(pl.multiple_of(start, MOE_ALIGN), R)
            xblk = xs_ref[rows, :]
            gblk = gs_ref[rows, :]
            lane = lax.broadcasted_iota(jnp.int32, gblk.shape, 1)
            hs = []
            for j in range(EXPERTS_PER_GROUP):
                e = grp * EXPERTS_PER_GROUP + j
                hgu = _dot(xblk, wgu_s[e])
                hg = hgu[:, :D_EXPERT]
                ge = jnp.sum(jnp.where(lane == e, gblk, 0.0), axis=-1, keepdims=True)
                hs.append((hg * jax.nn.sigmoid(hg) * hgu[:, D_EXPERT:] * ge).astype(BF16))
            ys_ref[rows, :] += _dot(jnp.concatenate(hs, axis=1), wd_s[grp])
            return 0

        lax.fori_loop(0, npairs_ref[tile], pair, 0)

        sorted_col = lax.broadcasted_iota(jnp.int32, (tm, nrows), 1).astype(F32)
        unperm = jnp.where(sorted_col == dest_col, 1.0, 0.0).astype(BF16)
        y = _dot(unperm, ys_ref[...].astype(BF16))
        o_ref[...] = _layer_norm(DEEPNORM_ALPHA * x + y, g_ref[...], b_ref[...])


def _moe_plan(counts):
    ntiles = counts.shape[0]
    max_blocks = -(-MOE_TM // MOE_R)
    padded = (counts + (MOE_ALIGN - 1)) // MOE_ALIGN * MOE_ALIGN
    seg_start = jnp.cumsum(padded, axis=1) - padded
    nblk = (counts + (MOE_R - 1)) // MOE_R
    j = jnp.arange(max_blocks, dtype=jnp.int32)[None, None, :]
    active = (j < nblk[:, :, None]).reshape(ntiles, N_GROUPS * max_blocks)
    order = jnp.argsort(jnp.logical_not(active), axis=1, stable=True)[:, :MOE_MAX_PAIRS].astype(jnp.int32)
    grp = order // max_blocks
    start = jnp.take_along_axis(seg_start, grp, axis=1) + (order % max_blocks) * MOE_R
    npairs = jnp.sum(active, axis=1).astype(jnp.int32)
    return npairs, start.astype(jnp.int32).reshape(-1), grp.reshape(-1)


def _moe_ln(x, gates, dest_rows, counts8, w_gate, w_up, w_down, layer, g, b):
    T = x.shape[0]
    tm = MOE_TM
    ntiles = T // tm
    counts = counts8.reshape(ntiles, 8, LANES)[:, 0, :N_GROUPS].astype(jnp.int32)
    npairs, pstart, pgrp = _moe_plan(counts)

    def tile_map(s, *_):
        return (jnp.maximum(s - N_EXPERTS, 0), 0)

    def w_map(s, *_):
        return (layer, jnp.minimum(s, N_EXPERTS - 1), 0, 0)

    const = lambda s, *_: (0, 0)
    grid_spec = pltpu.PrefetchScalarGridSpec(
        num_scalar_prefetch=3,
        grid=(N_EXPERTS + ntiles,),
        in_specs=[pl.BlockSpec((tm, D_MODEL), tile_map),
                  pl.BlockSpec((tm, LANES), tile_map),
                  pl.BlockSpec((8, tm), tile_map),
                  pl.BlockSpec((None, None, D_MODEL, D_EXPERT), w_map),
                  pl.BlockSpec((None, None, D_MODEL, D_EXPERT), w_map),
                  pl.BlockSpec((None, None, D_EXPERT, D_MODEL), w_map),
                  pl.BlockSpec((1, D_MODEL), const),
                  pl.BlockSpec((1, D_MODEL), const)],
        out_specs=pl.BlockSpec((tm, D_MODEL), tile_map),
        scratch_shapes=[pltpu.VMEM((N_EXPERTS, D_MODEL, 2 * D_EXPERT), BF16),
                        pltpu.VMEM((N_GROUPS, EXPERTS_PER_GROUP * D_EXPERT, D_MODEL), BF16),
                        pltpu.VMEM((MOE_SORT_ROWS, D_MODEL), BF16),
                        pltpu.VMEM((MOE_SORT_ROWS, LANES), F32),
                        pltpu.VMEM((MOE_SORT_ROWS, D_MODEL), F32)])
    return pl.pallas_call(
        _moe_kernel,
        grid_spec=grid_spec,
        out_shape=jax.ShapeDtypeStruct((T, D_MODEL), F32),
        compiler_params=_cparams(("arbitrary",), 56),
        name="moe_ln",
    )(npairs, pstart, pgrp, x, gates, dest_rows, w_gate, w_up, w_down,
      g.reshape(1, D_MODEL), b.reshape(1, D_MODEL))


def kernel(x, mem, a_w_in, a_lower_bounds, a_gnorm, b_w_in, w_kv_shared, w_mem_kv, w_o,
           ln_mix_g, ln_mix_b, ln_ffn_g, ln_ffn_b, w_group, b_group, w_router, b_router,
           w_gate, w_up, w_down):
    B, S, D = x.shape
    T = B * S
    xf = x.reshape(T, D)
    memf = mem.reshape(B * MEM_LEN, D)
    a_cols = a_w_in.shape[-1]
    b_cols = b_w_in.shape[-1]
    kv3 = None
    for layer in range(DEPTH):
        mem_kv = _matmul(memf, w_mem_kv, layer, 512, 2 * MEM_W, "mem_kv")
        mem_kv3 = mem_kv.reshape(B, MEM_LEN, 2 * MEM_W)
        if layer < N_A_LAYERS:
            h, f_raw = _in_proj_a(xf, a_w_in, layer, 256)
            h3 = h.reshape(B, S, a_cols)
            seq = _hgrn(h3, f_raw.reshape(B, S, SEQ_MIX_W), a_lower_bounds, a_gnorm[layer], layer)
            q_col_block = 4 * SEQ_MIX_W // MEM_W
        else:
            if layer == N_A_LAYERS:
                kv = _matmul(xf, w_kv_shared[None], 0, 512, 2 * SEQ_MIX_W, "kv_shared", BF16)
                kv3 = kv.reshape(B, S, 2 * SEQ_MIX_W)
            h = _matmul(xf, b_w_in, layer - N_A_LAYERS, 512, b_cols, "in_proj_b", BF16)
            h3 = h.reshape(B, S, b_cols)
            seq = _sb_attention(h3, kv3)
            q_col_block = SEQ_MIX_W // MEM_W
        memo = _mem_attention(h3, mem_kv3, q_col_block, 2048)
        x1, gates, dest_rows, counts8 = _oproj_ln_route(
            seq.reshape(T, SEQ_MIX_W), memo.reshape(T, MEM_W), xf, w_o, layer,
            ln_mix_g[layer], ln_mix_b[layer],
            w_group[layer], b_group[layer], w_router[layer], b_router[layer])
        xf = _moe_ln(x1, gates, dest_rows, counts8, w_gate, w_up, w_down, layer,
                     ln_ffn_g[layer], ln_ffn_b[layer])
    return xf.reshape(B, S, D)
```

```python
import functools
import math

import numpy as np
import jax
import jax.numpy as jnp
from jax import lax
from jax.experimental import pallas as pl
from jax.experimental.pallas import tpu as pltpu

F32 = jnp.float32
BF16 = jnp.bfloat16

D_MODEL = 1024
DEPTH = 4
N_A_LAYERS = DEPTH // 2
MEM_LEN = 256
MEM_HEADS = 4
HEAD_DIM = 64
MEM_W = MEM_HEADS * HEAD_DIM
SEQ_MIX_W = D_MODEL - MEM_W
HGRN_DK = 128
HGRN_HEADS = SEQ_MIX_W // HGRN_DK
SB_HEADS = SEQ_MIX_W // HEAD_DIM
N_GROUPS = 4
EXPERTS_PER_GROUP = 4
N_EXPERTS = N_GROUPS * EXPERTS_PER_GROUP
D_EXPERT = 256
DEEPNORM_ALPHA = (2 * DEPTH) ** 0.25
LN_EPS = 1e-5
RMS_EPS = 1e-6

LANES = 128
HGRN_C = 256
SB_TK = 256
SB_ZERO_MASS = 110.0
MOE_TM = 512
MOE_R = 144
MOE_ALIGN = 16
MOE_MAX_PAIRS = MOE_TM // MOE_R + N_GROUPS
MOE_SORT_ROWS = -(-(MOE_TM + N_GROUPS * MOE_ALIGN + MOE_R) // 256) * 256
MOE_DEST_LANE = N_EXPERTS
OPROJ_TILES = 2
MIB = 1024 * 1024


def _cparams(semantics, vmem_mib):
    return pltpu.CompilerParams(dimension_semantics=semantics, vmem_limit_bytes=vmem_mib * MIB)


def _dot(a, b):
    return jnp.dot(a, b, preferred_element_type=F32)


def _dot_nt(a, b):
    return lax.dot_general(a, b, (((1,), (1,)), ((), ())), preferred_element_type=F32)


def _dot_tn(a, b):
    return lax.dot_general(a, b, (((0,), (0,)), ((), ())), preferred_element_type=F32)


def _split_bf16(x):
    hi = x.astype(BF16)
    lo = (x - hi.astype(F32)).astype(BF16)
    return hi, lo


def _layer_norm(r, g, b):
    mu = jnp.mean(r, axis=-1, keepdims=True)
    d = r - mu
    var = jnp.mean(d * d, axis=-1, keepdims=True)
    return d * lax.rsqrt(var + LN_EPS) * g + b


def _mm_kernel(x_ref, w_ref, o_ref, wb_ref):
    @pl.when(pl.program_id(1) == 0)
    def _():
        wb_ref[...] = w_ref[...].astype(BF16)

    o_ref[...] = _dot(x_ref[...].astype(BF16), wb_ref[...]).astype(o_ref.dtype)


def _matmul(x, w_stack, layer, tm, tn, name, out_dtype=F32):
    M, K = x.shape
    N = w_stack.shape[2]
    assert M % tm == 0 and N % tn == 0
    return pl.pallas_call(
        _mm_kernel,
        grid=(N // tn, M // tm),
        in_specs=[pl.BlockSpec((tm, K), lambda j, i: (i, 0)),
                  pl.BlockSpec((None, K, tn), lambda j, i: (layer, 0, j))],
        out_specs=pl.BlockSpec((tm, tn), lambda j, i: (i, j)),
        out_shape=jax.ShapeDtypeStruct((M, N), out_dtype),
        scratch_shapes=[pltpu.VMEM((K, tn), BF16)],
        compiler_params=_cparams(("parallel", "arbitrary"), 48),
        name=name,
    )(x, w_stack)


def _in_proj_a_kernel(x_ref, w_ref, o_ref, f_ref, wb_ref):
    @pl.when(pl.program_id(0) == 0)
    def _():
        wb_ref[...] = w_ref[...].astype(BF16)

    acc = _dot(x_ref[...].astype(BF16), wb_ref[...])
    o_ref[...] = acc.astype(BF16)
    f_ref[...] = acc[:, SEQ_MIX_W:2 * SEQ_MIX_W]


def _in_proj_a(x, w_stack, layer, tm):
    M, K = x.shape
    N = w_stack.shape[2]
    return pl.pallas_call(
        _in_proj_a_kernel,
        grid=(M // tm,),
        in_specs=[pl.BlockSpec((tm, K), lambda i: (i, 0)),
                  pl.BlockSpec((None, K, N), lambda i: (layer, 0, 0))],
        out_specs=[pl.BlockSpec((tm, N), lambda i: (i, 0)),
                   pl.BlockSpec((tm, SEQ_MIX_W), lambda i: (i, 0))],
        out_shape=[jax.ShapeDtypeStruct((M, N), BF16),
                   jax.ShapeDtypeStruct((M, SEQ_MIX_W), F32)],
        scratch_shapes=[pltpu.VMEM((K, N), BF16)],
        compiler_params=_cparams(("arbitrary",), 52),
        name="in_proj_a",
    )(x, w_stack)


def _hgrn_levels():
    m = HGRN_C // 2
    out = []
    while m >= 1:
        out.append(m)
        m //= 2
    return out


def _hgrn_level_decay(m, G, f, row):
    C = HGRN_C
    if m >= 4:
        G3 = G.reshape(C // (2 * m), 2 * m, LANES)
        return jnp.exp(-jnp.abs(G3 - G3[:, m - 1:m, :])).reshape(C, LANES)
    if m == 2:
        f_next = pltpu.roll(f, C - 1, axis=0)
        f_prev = pltpu.roll(f, 1, axis=0)
        p = row & 3
        return jnp.where(p == 0, f_next, jnp.where(p == 1, 1.0, jnp.where(p == 2, f, f * f_prev)))
    return jnp.where((row & 1) == 1, f, 1.0)


def _hgrn_kernel(layer, q_ref, f_ref, i_ref, g_ref, lbp_ref, gn_ref, incl_ref, o_ref, st_ref):
    C = HGRN_C

    @pl.when(pl.program_id(1) == 0)
    def _():
        st_ref[...] = jnp.zeros_like(st_ref)

    a = lbp_ref[...]
    ea = jnp.exp(a - jnp.max(a, axis=0, keepdims=True))
    pa = ea / jnp.sum(ea, axis=0, keepdims=True)
    lb_all = jnp.zeros((1, SEQ_MIX_W), F32)
    for l in range(1, layer + 1):
        lb_all = lb_all + pa[l:l + 1]

    t_idx = lax.broadcasted_iota(jnp.int32, (C, C), 0)
    s_idx = lax.broadcasted_iota(jnp.int32, (C, C), 1)
    split = t_idx ^ s_idx
    levels = _hgrn_levels()
    incl = incl_ref[...]
    row = lax.broadcasted_iota(jnp.int32, (C, LANES), 0)

    for h in range(HGRN_HEADS):
        lanes = slice(h * LANES, (h + 1) * LANES)
        lb = lb_all[:, lanes]
        fr = f_ref[:, lanes]
        logsig = jnp.minimum(fr, 0.0) - jnp.log(1.0 + jnp.exp(-jnp.abs(fr)))
        la = jnp.log(lb)
        lc = jnp.log1p(-lb) + logsig
        logf = jnp.maximum(la, lc) + jnp.log(1.0 + jnp.exp(-jnp.abs(la - lc)))
        f = jnp.exp(logf)
        k = 1.0 - f

        hi = logf.astype(BF16)
        r1 = logf - hi.astype(F32)
        mid = r1.astype(BF16)
        lo = (r1 - mid.astype(F32)).astype(BF16)
        g3 = _dot(incl, jnp.concatenate([hi, mid, lo], axis=1))
        G = g3[:, :LANES] + g3[:, LANES:2 * LANES] + g3[:, 2 * LANES:]
        dec_g = jnp.exp(G)
        dec_suf = jnp.exp(G[C - 1:C, :] - G)

        qr = q_ref[:, lanes].astype(F32)
        q = qr * jax.nn.sigmoid(qr) * (HGRN_DK ** -0.5)
        v = i_ref[:, lanes].astype(BF16)

        sc = None
        for li in range(len(levels) - 1, -1, -1):
            m = levels[li]
            u = (jnp.where((row & m) != 0, q, k) * _hgrn_level_decay(m, G, f, row)).astype(BF16)
            p = _dot_nt(u, u)
            sc = p if sc is None else jnp.where(split >= m, p, sc)
        sc = jnp.where(s_idx < t_idx, sc, 0.0)
        sc = jnp.where(s_idx == t_idx, jnp.sum(q * k, axis=-1, keepdims=True), sc)

        st = st_ref[h]
        o = _dot(sc.astype(BF16), v) + _dot_nt((q * dec_g).astype(BF16), st.astype(BF16))
        st_ref[h] = st * dec_g[C - 1:C, :] + _dot_tn(v, (k * dec_suf).astype(BF16))

        o = o * lax.rsqrt(jnp.mean(o * o, axis=-1, keepdims=True) + RMS_EPS)
        gr = g_ref[:, lanes].astype(F32)
        o_ref[:, lanes] = (o * gn_ref[:, lanes] * (gr * jax.nn.sigmoid(gr))).astype(o_ref.dtype)


def _hgrn(h3, f3, a_lower_bounds, gnorm, layer):
    B, S, _ = h3.shape
    C = HGRN_C
    W = SEQ_MIX_W
    incl = jnp.asarray(np.arange(C)[None, :] <= np.arange(C)[:, None], dtype=BF16)

    def col(j):
        return pl.BlockSpec((None, C, W), lambda b, c: (b, c, j))

    return pl.pallas_call(
        functools.partial(_hgrn_kernel, layer),
        grid=(B, S // C),
        in_specs=[col(0), col(0), col(2), col(3),
                  pl.BlockSpec((N_A_LAYERS, W), lambda b, c: (0, 0)),
                  pl.BlockSpec((1, W), lambda b, c: (0, 0)),
                  pl.BlockSpec((C, C), lambda b, c: (0, 0))],
        out_specs=pl.BlockSpec((None, C, W), lambda b, c: (b, c, 0)),
        out_shape=jax.ShapeDtypeStruct((B, S, W), BF16),
        scratch_shapes=[pltpu.VMEM((HGRN_HEADS, LANES, HGRN_DK), F32)],
        compiler_params=_cparams(("parallel", "arbitrary"), 32),
        name="hgrn2",
    )(h3, f3, h3, h3, a_lower_bounds, gnorm.reshape(1, W), incl)


def _sb_kernel(q_ref, k_ref, v_ref, later_ref, o_ref, acc_ref, carry_ref):
    TK = SB_TK
    TQ = 2 * TK
    qi = pl.program_id(2)
    lane = lax.broadcasted_iota(jnp.int32, (TQ, LANES), 1)
    q = q_ref[...] * (1.0 / math.sqrt(HEAD_DIM))
    zero = jnp.zeros_like(q)
    qh = [jnp.where(lane < HEAD_DIM, q, zero), jnp.where(lane >= HEAD_DIM, q, zero)]
    later = later_ref[...]
    row = lax.broadcasted_iota(jnp.int32, (TK, TK), 0)
    col = lax.broadcasted_iota(jnp.int32, (TK, TK), 1)
    causal = col < row

    def sweep(h, rows, kb, masked, carry):
        start = pl.multiple_of(kb * TK, TK)
        z = _dot_nt(qh[h][rows], k_ref[pl.ds(start, TK), :])
        sp = jnp.maximum(z, 0.0) + jnp.log(1.0 + jnp.exp(-jnp.abs(z)))
        spm = jnp.where(causal, sp, 0.0) if masked else sp
        rest = _dot(spm.astype(BF16), later) + carry
        a = jnp.exp(z - sp - rest)
        if masked:
            a = jnp.where(causal, a, 0.0)
        contrib = _dot(a.astype(BF16), v_ref[pl.ds(start, TK), :])
        return contrib, carry + jnp.sum(spm, axis=1, keepdims=True)

    top = slice(0, TK)
    bot = slice(TK, TQ)
    no_mass = jnp.zeros((TK, 1), F32)
    has_prev = qi > 0
    for h in range(2):
        acc_t, c_t = sweep(h, top, 2 * qi, True, no_mass)
        acc_t2, c_t2 = sweep(h, top, jnp.maximum(2 * qi - 1, 0), False, c_t)
        acc_b, c_b = sweep(h, bot, 2 * qi + 1, True, no_mass)
        acc_b2, c_b = sweep(h, bot, 2 * qi, False, c_b)
        acc_ref[h, top, :] = acc_t + jnp.where(has_prev, acc_t2, 0.0)
        acc_ref[h, bot, :] = acc_b + acc_b2
        carry_ref[h, top, :] = jnp.where(has_prev, c_t2, c_t)
        carry_ref[h, bot, :] = c_b

    def min_mass():
        return jnp.minimum(jnp.min(carry_ref[0]), jnp.min(carry_ref[1]))

    def more(state):
        j, mass = state
        return jnp.logical_and(j < 2 * qi, mass < SB_ZERO_MASS)

    def earlier(state):
        j, _ = state
        kb_bot = 2 * qi - 1 - j
        kb_top = kb_bot - 1
        top_live = kb_top >= 0
        for h in range(2):
            contrib, c = sweep(h, top, jnp.maximum(kb_top, 0), False, carry_ref[h, top, :])
            acc_ref[h, top, :] += jnp.where(top_live, contrib, 0.0)
            carry_ref[h, top, :] = jnp.where(top_live, c, carry_ref[h, top, :])
            contrib, c = sweep(h, bot, kb_bot, False, carry_ref[h, bot, :])
            acc_ref[h, bot, :] += contrib
            carry_ref[h, bot, :] = c
        return j + 1, min_mass()

    lax.while_loop(more, earlier, (jnp.int32(0), min_mass()))
    o_ref[...] = jnp.where(lane < HEAD_DIM, acc_ref[0], acc_ref[1]).astype(o_ref.dtype)


def _sb_attention(h3, kv3):
    B, S, _ = h3.shape
    TK = SB_TK
    TQ = 2 * TK
    pairs = SEQ_MIX_W // LANES
    later = jnp.asarray(np.arange(TK)[:, None] > np.arange(TK)[None, :], dtype=BF16)
    return pl.pallas_call(
        _sb_kernel,
        grid=(B, pairs, S // TQ),
        in_specs=[pl.BlockSpec((None, TQ, LANES), lambda b, p, i: (b, i, p)),
                  pl.BlockSpec((None, S, LANES), lambda b, p, i: (b, 0, p)),
                  pl.BlockSpec((None, S, LANES), lambda b, p, i: (b, 0, pairs + p)),
                  pl.BlockSpec((TK, TK), lambda b, p, i: (0, 0))],
        out_specs=pl.BlockSpec((None, TQ, LANES), lambda b, p, i: (b, i, p)),
        out_shape=jax.ShapeDtypeStruct((B, S, SEQ_MIX_W), BF16),
        scratch_shapes=[pltpu.VMEM((2, TQ, LANES), F32), pltpu.VMEM((2, TQ, 1), F32)],
        compiler_params=_cparams(("parallel", "parallel", "arbitrary"), 32),
        name="stick_breaking",
    )(h3, kv3, kv3, later)


def _mem_kernel(q_ref, kv_ref, o_ref):
    q = q_ref[...] * (1.0 / math.sqrt(HEAD_DIM))
    outs = []
    for h in range(MEM_HEADS):
        lanes = slice(h * HEAD_DIM, (h + 1) * HEAD_DIM)
        vl = slice(MEM_W + h * HEAD_DIM, MEM_W + (h + 1) * HEAD_DIM)
        s = _dot_nt(q[:, lanes].astype(BF16), kv_ref[:, lanes].astype(BF16))
        e = jnp.exp(s - jnp.max(s, axis=-1, keepdims=True))
        den = jnp.sum(e, axis=-1, keepdims=True)
        outs.append(_dot(e.astype(BF16), kv_ref[:, vl].astype(BF16)) / den)
    o_ref[...] = jnp.concatenate(outs, axis=1).astype(o_ref.dtype)


def _mem_attention(h3, mem_kv3, q_col_block, tm):
    B, S, _ = h3.shape
    return pl.pallas_call(
        _mem_kernel,
        grid=(B, S // tm),
        in_specs=[pl.BlockSpec((None, tm, MEM_W), lambda b, i: (b, i, q_col_block)),
                  pl.BlockSpec((None, MEM_LEN, 2 * MEM_W), lambda b, i: (b, 0, 0))],
        out_specs=pl.BlockSpec((None, tm, MEM_W), lambda b, i: (b, i, 0)),
        out_shape=jax.ShapeDtypeStruct((B, S, MEM_W), BF16),
        compiler_params=_cparams(("parallel", "parallel"), 32),
        name="mem_attention",
    )(h3, mem_kv3)


def _route(lg):
    lane = lax.broadcasted_iota(jnp.int32, lg.shape, 1)
    lane_f = lane.astype(F32)
    neg = -jnp.inf
    far = float(LANES)

    lgg = jnp.where((lane >= N_EXPERTS) & (lane < N_EXPERTS + N_GROUPS), lg, neg)
    gmax = jnp.max(lgg, axis=-1, keepdims=True)
    p_top = 1.0 / jnp.sum(jnp.exp(lgg - gmax), axis=-1, keepdims=True)
    g_lane = jnp.min(jnp.where(lgg == gmax, lane_f, far), axis=-1, keepdims=True)
    first = (g_lane - float(N_EXPERTS)) * float(EXPERTS_PER_GROUP)

    le = jnp.where((lane_f >= first) & (lane_f < first + float(EXPERTS_PER_GROUP)), lg, neg)
    m1 = jnp.max(le, axis=-1, keepdims=True)
    i1 = jnp.min(jnp.where(le == m1, lane_f, far), axis=-1, keepdims=True)
    le2 = jnp.where(lane_f == i1, neg, le)
    m2 = jnp.max(le2, axis=-1, keepdims=True)
    i2 = jnp.min(jnp.where(le2 == m2, lane_f, far), axis=-1, keepdims=True)
    e2 = jnp.exp(m2 - m1)
    den = 1.0 + e2
    gates = p_top * (jnp.where(lane_f == i1, 1.0 / den, 0.0) + jnp.where(lane_f == i2, e2 / den, 0.0))
    return gates, g_lane - float(N_EXPERTS)


def _oproj_kernel(seq_ref, mem_ref, x_ref, wo_ref, g_ref, b_ref, wr_ref, br_ref, before_ref,
                  o_ref, gates_ref, drow_ref, cnt_ref, wob_ref, wrh_ref, wrl_ref):
    @pl.when(pl.program_id(0) == 0)
    def _():
        wob_ref[...] = wo_ref[...].astype(BF16)
        hi, lo = _split_bf16(wr_ref[...])
        wrh_ref[...] = hi
        wrl_ref[...] = lo

    y = _dot(seq_ref[...], wob_ref[0:SEQ_MIX_W, :]) + _dot(mem_ref[...], wob_ref[SEQ_MIX_W:, :])
    x1 = _layer_norm(DEEPNORM_ALPHA * x_ref[...] + y, g_ref[...], b_ref[...])
    o_ref[...] = x1
    xh, xl = _split_bf16(x1)
    wrh = wrh_ref[...]
    lg2 = _dot(xh, jnp.concatenate([wrh, wrl_ref[...]], axis=1))
    gates_all, grp_all = _route(lg2[:, :LANES] + lg2[:, LANES:] + _dot(xl, wrh) + br_ref[...])

    for t in range(OPROJ_TILES):
        rows = slice(t * MOE_TM, (t + 1) * MOE_TM)
        meta = slice(8 * t, 8 * (t + 1))
        gates = gates_all[rows]
        grp = grp_all[rows]

        lane_f = lax.broadcasted_iota(jnp.int32, gates.shape, 1).astype(F32)
        onehot = jnp.where(lane_f == grp, 1.0, 0.0)
        ranks = _dot(before_ref[...], onehot.astype(BF16))
        rank = jnp.sum(jnp.where(lane_f == grp, ranks, 0.0), axis=-1, keepdims=True)
        counts = jnp.sum(onehot, axis=0, keepdims=True)
        padded = jnp.floor((counts + float(MOE_ALIGN - 1)) * (1.0 / MOE_ALIGN)) * float(MOE_ALIGN)
        offset = jnp.sum(jnp.where(lane_f < grp, padded, 0.0), axis=-1, keepdims=True)
        dest = offset + rank
        gates_ref[rows, :] = gates + jnp.where(lane_f == float(MOE_DEST_LANE), dest, 0.0)
        cnt_ref[meta, :] = jnp.broadcast_to(counts, (8, LANES))
        hi16 = jnp.floor(dest * (1.0 / 16.0))
        digits = (jnp.where(lane_f == 0.0, hi16, 0.0)
                  + jnp.where(lane_f == 1.0, dest - 16.0 * hi16, 0.0))
        sel_lane = lax.broadcasted_iota(jnp.int32, (8, LANES), 1)
        sel = jnp.where(sel_lane == 0, 16.0, jnp.where(sel_lane == 1, 1.0, 0.0)).astype(BF16)
        drow_ref[meta, :] = _dot_nt(sel, digits.astype(BF16))


def _oproj_ln_route(seq, memo, x, w_o, layer, g, b, w_group, b_group, w_router, b_router):
    T = x.shape[0]
    tm = MOE_TM
    step_rows = OPROJ_TILES * tm
    pad = LANES - N_EXPERTS - N_GROUPS
    wr = jnp.concatenate([w_router, w_group, jnp.zeros((D_MODEL, pad), F32)], axis=1)
    br = jnp.concatenate([b_router, b_group, jnp.zeros((pad,), F32)]).reshape(1, LANES)
    before = jnp.asarray(np.arange(tm)[None, :] < np.arange(tm)[:, None], dtype=BF16)
    const = lambda i: (0, 0)
    return pl.pallas_call(
        _oproj_kernel,
        grid=(T // step_rows,),
        in_specs=[pl.BlockSpec((step_rows, SEQ_MIX_W), lambda i: (i, 0)),
                  pl.BlockSpec((step_rows, MEM_W), lambda i: (i, 0)),
                  pl.BlockSpec((step_rows, D_MODEL), lambda i: (i, 0)),
                  pl.BlockSpec((None, D_MODEL, D_MODEL), lambda i: (layer, 0, 0)),
                  pl.BlockSpec((1, D_MODEL), const),
                  pl.BlockSpec((1, D_MODEL), const),
                  pl.BlockSpec((D_MODEL, LANES), const),
                  pl.BlockSpec((1, LANES), const),
                  pl.BlockSpec((tm, tm), const)],
        out_specs=[pl.BlockSpec((step_rows, D_MODEL), lambda i: (i, 0)),
                   pl.BlockSpec((step_rows, LANES), lambda i: (i, 0)),
                   pl.BlockSpec((8 * OPROJ_TILES, tm), lambda i: (i, 0)),
                   pl.BlockSpec((8 * OPROJ_TILES, LANES), lambda i: (i, 0))],
        out_shape=[jax.ShapeDtypeStruct((T, D_MODEL), F32),
                   jax.ShapeDtypeStruct((T, LANES), F32),
                   jax.ShapeDtypeStruct((8 * (T // tm), tm), F32),
                   jax.ShapeDtypeStruct((8 * (T // tm), LANES), F32)],
        scratch_shapes=[pltpu.VMEM((D_MODEL, D_MODEL), BF16),
                        pltpu.VMEM((D_MODEL, LANES), BF16),
                        pltpu.VMEM((D_MODEL, LANES), BF16)],
        compiler_params=_cparams(("arbitrary",), 48),
        name="oproj_ln_route",
    )(seq, memo, x, w_o, g.reshape(1, D_MODEL), b.reshape(1, D_MODEL), wr, br, before)


def _moe_kernel(npairs_ref, pstart_ref, pgrp_ref,
                x_ref, gates_ref, drow_ref, wg_ref, wu_ref, wd_ref, g_ref, b_ref, o_ref,
                wgu_s, wd_s, xs_ref, gs_ref, ys_ref):
    step = pl.program_id(0)
    tm = MOE_TM
    R = MOE_R
    nrows = MOE_SORT_ROWS

    @pl.when(step < N_EXPERTS)
    def _():
        wgu_s[step, :, 0:D_EXPERT] = wg_ref[...].astype(BF16)
        wgu_s[step, :, D_EXPERT:] = wu_ref[...].astype(BF16)
        row0 = pl.multiple_of((step % EXPERTS_PER_GROUP) * D_EXPERT, D_EXPERT)
        wd_s[step // EXPERTS_PER_GROUP, pl.ds(row0, D_EXPERT), :] = wd_ref[...].astype(BF16)

    @pl.when(step >= N_EXPERTS)
    def _():
        tile = step - N_EXPERTS
        x = x_ref[...]
        gates = gates_ref[...]
        lane_g = lax.broadcasted_iota(jnp.int32, gates.shape, 1)
        dest_col = jnp.sum(jnp.where(lane_g == MOE_DEST_LANE, gates, 0.0), axis=-1, keepdims=True)
        used = MOE_TM + N_GROUPS * MOE_ALIGN
        sorted_row = lax.broadcasted_iota(jnp.int32, (used, tm), 0).astype(F32)
        perm = jnp.where(sorted_row == drow_ref[0:1, :], 1.0, 0.0).astype(BF16)
        xs_ref[0:used, :] = _dot(perm, x.astype(BF16)).astype(BF16)
        xs_ref[used:, :] = jnp.zeros((nrows - used, D_MODEL), BF16)
        ghi, glo = _split_bf16(gates)
        g2 = _dot(perm, jnp.concatenate([ghi, glo], axis=1))
        gs_ref[0:used, :] = g2[:, :LANES] + g2[:, LANES:]
        gs_ref[used:, :] = jnp.zeros((nrows - used, LANES), F32)
        ys_ref[...] = jnp.zeros_like(ys_ref)

        def pair(p, _):
            start = pstart_ref[tile * MOE_MAX_PAIRS + p]
            grp = pgrp_ref[tile * MOE_MAX_PAIRS + p]
            rows = pl.ds(pl.multiple_of(start, MOE_ALIGN), R)
            xblk = xs_ref[rows, :]
            gblk = gs_ref[rows, :]
            lane = lax.broadcasted_iota(jnp.int32, gblk.shape, 1)
            hs = []
            for j in range(EXPERTS_PER_GROUP):
                e = grp * EXPERTS_PER_GROUP + j
                hgu = _dot(xblk, wgu_s[e])
                hg = hgu[:, :D_EXPERT]
                ge = jnp.sum(jnp.where(lane == e, gblk, 0.0), axis=-1, keepdims=True)
                hs.append((hg * jax.nn.sigmoid(hg) * hgu[:, D_EXPERT:] * ge).astype(BF16))
            ys_ref[rows, :] += _dot(jnp.concatenate(hs, axis=1), wd_s[grp])
            return 0

        lax.fori_loop(0, npairs_ref[tile], pair, 0)

        sorted_col = lax.broadcasted_iota(jnp.int32, (tm, nrows), 1).astype(F32)
        unperm = jnp.where(sorted_col == dest_col, 1.0, 0.0).astype(BF16)
        y = _dot(unperm, ys_ref[...].astype(BF16))
        o_ref[...] = _layer_norm(DEEPNORM_ALPHA * x + y, g_ref[...], b_ref[...])


def _moe_plan(counts):
    ntiles = counts.shape[0]
    max_blocks = -(-MOE_TM // MOE_R)
    padded = (counts + (MOE_ALIGN - 1)) // MOE_ALIGN * MOE_ALIGN
    seg_start = jnp.cumsum(padded, axis=1) - padded
    nblk = (counts + (MOE_R - 1)) // MOE_R
    j = jnp.arange(max_blocks, dtype=jnp.int32)[None, None, :]
    active = (j < nblk[:, :, None]).reshape(ntiles, N_GROUPS * max_blocks)
    order = jnp.argsort(jnp.logical_not(active), axis=1, stable=True)[:, :MOE_MAX_PAIRS].astype(jnp.int32)
    grp = order // max_blocks
    start = jnp.take_along_axis(seg_start, grp, axis=1) + (order % max_blocks) * MOE_R
    npairs = jnp.sum(active, axis=1).astype(jnp.int32)
    return npairs, start.astype(jnp.int32).reshape(-1), grp.reshape(-1)


def _moe_ln(x, gates, dest_rows, counts8, w_gate, w_up, w_down, layer, g, b):
    T = x.shape[0]
    tm = MOE_TM
    ntiles = T // tm
    counts = counts8.reshape(ntiles, 8, LANES)[:, 0, :N_GROUPS].astype(jnp.int32)
    npairs, pstart, pgrp = _moe_plan(counts)

    def tile_map(s, *_):
        return (jnp.maximum(s - N_EXPERTS, 0), 0)

    def w_map(s, *_):
        return (layer, jnp.minimum(s, N_EXPERTS - 1), 0, 0)

    const = lambda s, *_: (0, 0)
    grid_spec = pltpu.PrefetchScalarGridSpec(
        num_scalar_prefetch=3,
        grid=(N_EXPERTS + ntiles,),
        in_specs=[pl.BlockSpec((tm, D_MODEL), tile_map),
                  pl.BlockSpec((tm, LANES), tile_map),
                  pl.BlockSpec((8, tm), tile_map),
                  pl.BlockSpec((None, None, D_MODEL, D_EXPERT), w_map),
                  pl.BlockSpec((None, None, D_MODEL, D_EXPERT), w_map),
                  pl.BlockSpec((None, None, D_EXPERT, D_MODEL), w_map),
                  pl.BlockSpec((1, D_MODEL), const),
                  pl.BlockSpec((1, D_MODEL), const)],
        out_specs=pl.BlockSpec((tm, D_MODEL), tile_map),
        scratch_shapes=[pltpu.VMEM((N_EXPERTS, D_MODEL, 2 * D_EXPERT), BF16),
                        pltpu.VMEM((N_GROUPS, EXPERTS_PER_GROUP * D_EXPERT, D_MODEL), BF16),
                        pltpu.VMEM((MOE_SORT_ROWS, D_MODEL), BF16),
                        pltpu.VMEM((MOE_SORT_ROWS, LANES), F32),
                        pltpu.VMEM((MOE_SORT_ROWS, D_MODEL), F32)])
    return pl.pallas_call(
        _moe_kernel,
        grid_spec=grid_spec,
        out_shape=jax.ShapeDtypeStruct((T, D_MODEL), F32),
        compiler_params=_cparams(("arbitrary",), 56),
        name="moe_ln",
    )(npairs, pstart, pgrp, x, gates, dest_rows, w_gate, w_up, w_down,
      g.reshape(1, D_MODEL), b.reshape(1, D_MODEL))


def kernel(x, mem, a_w_in, a_lower_bounds, a_gnorm, b_w_in, w_kv_shared, w_mem_kv, w_o,
           ln_mix_g, ln_mix_b, ln_ffn_g, ln_ffn_b, w_group, b_group, w_router, b_router,
           w_gate, w_up, w_down):
    B, S, D = x.shape
    T = B * S
    xf = x.reshape(T, D)
    memf = mem.reshape(B * MEM_LEN, D)
    a_cols = a_w_in.shape[-1]
    b_cols = b_w_in.shape[-1]
    kv3 = None
    for layer in range(DEPTH):
        mem_kv = _matmul(memf, w_mem_kv, layer, 512, 2 * MEM_W, "mem_kv")
        mem_kv3 = mem_kv.reshape(B, MEM_LEN, 2 * MEM_W)
        if layer < N_A_LAYERS:
            h, f_raw = _in_proj_a(xf, a_w_in, layer, 256)
            h3 = h.reshape(B, S, a_cols)
            seq = _hgrn(h3, f_raw.reshape(B, S, SEQ_MIX_W), a_lower_bounds, a_gnorm[layer], layer)
            q_col_block = 4 * SEQ_MIX_W // MEM_W
        else:
            if layer == N_A_LAYERS:
                kv = _matmul(xf, w_kv_shared[None], 0, 512, 2 * SEQ_MIX_W, "kv_shared", BF16)
                kv3 = kv.reshape(B, S, 2 * SEQ_MIX_W)
            h = _matmul(xf, b_w_in, layer - N_A_LAYERS, 512, b_cols, "in_proj_b", BF16)
            h3 = h.reshape(B, S, b_cols)
            seq = _sb_attention(h3, kv3)
            q_col_block = SEQ_MIX_W // MEM_W
        memo = _mem_attention(h3, mem_kv3, q_col_block, 2048)
        x1, gates, dest_rows, counts8 = _oproj_ln_route(
            seq.reshape(T, SEQ_MIX_W), memo.reshape(T, MEM_W), xf, w_o, layer,
            ln_mix_g[layer], ln_mix_b[layer],
            w_group[layer], b_group[layer], w_router[layer], b_router[layer])
        xf = _moe_ln(x1, gates, dest_rows, counts8, w_gate, w_up, w_down, layer,
                     ln_ffn_g[layer], ln_ffn_b[layer])
    return xf.reshape(B, S, D)
```
